```python
import jax, jax.numpy as jnp
from jax import lax
import numpy as np

D_MODEL = 2048
BATCH = 4
SEQ = 8192
DEPTH = 1
DEC_BATCH = 16
DEC_SEQ = 16
PAST_LEN = 4096

CHUNK = 64
Q_BLOCK = 128
D_A = D_MODEL // 2
D_B = D_MODEL - D_A
HEAD_DIM_A = 128
N_HEADS_A = D_A // HEAD_DIM_A
N_HEADS_B = 4
HEAD_DIM_B = D_B // N_HEADS_B
CONV_W = 4
D_FF = 4 * D_MODEL
D_IN = 3 * D_A + 4 * D_B + 2 * N_HEADS_B
EPS = 1e-6

kernel_name = 'hymba_stickbreak_mlstm_stream_step'


def _rmsnorm(x, g):
    xf = x.astype(jnp.float32)
    y = xf * lax.rsqrt(jnp.mean(xf * xf, axis=-1, keepdims=True) + EPS)
    return (y * g.astype(jnp.float32)).astype(x.dtype)


def _split_cols(u):
    sizes = (D_A, D_A, D_A, D_B, D_B, D_B, D_B, N_HEADS_B, N_HEADS_B)
    return jnp.split(u, np.cumsum(sizes)[:-1].tolist(), axis=-1)


def _sb_block(q, k, v, q_pos, k_pos):
    z = jnp.einsum('bhtd,bhsd->bhts', q, k).astype(jnp.float32) * (HEAD_DIM_A ** -0.5)
    mask = k_pos[None, :] < q_pos[:, None]
    log_beta = jax.nn.log_sigmoid(z)
    log_1m = jnp.where(mask, log_beta - z, 0.0)
    tail = lax.cumsum(log_1m, axis=3, reverse=True) - log_1m
    a = jnp.where(mask, jnp.exp(log_beta + tail), 0.0)
    return jnp.einsum('bhts,bhsd->bhtd', a.astype(v.dtype), v)


def _sb_attention(q, k, v, q_pos, k_pos):
    B, H, T, d = q.shape
    blk = min(Q_BLOCK, T)
    nb = T // blk
    qb = jnp.moveaxis(q.reshape(B, H, nb, blk, d), 2, 0)
    out = lax.map(lambda a: _sb_block(a[0], k, v, a[1], k_pos), (qb, q_pos.reshape(nb, blk)))
    return jnp.moveaxis(out, 0, 2).reshape(B, H, T, d)


def _mlstm_chunk(state, q, k, v, ig, lf):
    C, n, m = state
    L = q.shape[2]
    b = jnp.cumsum(lf, axis=-1)
    causal = jnp.tril(jnp.ones((L, L), dtype=bool))
    dmat = jnp.where(causal, b[..., :, None] - b[..., None, :] + ig[..., None, :], -jnp.inf)
    inter = b + m[..., None]
    m_t = jnp.maximum(inter, jnp.max(dmat, axis=-1))
    w_inter = jnp.exp(inter - m_t)
    s = jnp.exp(dmat - m_t[..., None]) * jnp.einsum('bhtd,bhsd->bhts', q, k)
    num = w_inter[..., None] * jnp.einsum('bhtk,bhkv->bhtv', q, C) + jnp.einsum('bhts,bhsv->bhtv', s, v)
    den = w_inter * jnp.einsum('bhtk,bhk->bht', q, n) + jnp.sum(s, axis=-1)
    h = num / jnp.maximum(jnp.abs(den), jnp.exp(-m_t))[..., None]
    m_new = m_t[..., -1]
    w_state = jnp.exp(b[..., -1:] - b + ig - m_new[..., None])
    decay = jnp.exp(b[..., -1] + m - m_new)
    C_new = decay[..., None, None] * C + jnp.einsum('bhs,bhsk,bhsv->bhkv', w_state, k, v)
    n_new = decay[..., None] * n + jnp.einsum('bhs,bhsk->bhk', w_state, k)
    return h, (C_new, n_new, m_new)


def _mlstm(q, k, v, ig, lf, state):
    B, H, T, _ = q.shape
    L = min(CHUNK, T)
    nc = T // L

    def chunks(t):
        return jnp.moveaxis(t.reshape((B, H, nc, L) + t.shape[3:]), 2, 0)

    def step(carry, xs):
        h, carry = _mlstm_chunk(carry, *xs)
        return carry, h

    state, hs = lax.scan(step, state, (chunks(q), chunks(k), chunks(v), chunks(ig), chunks(lf)))
    return jnp.moveaxis(hs, 0, 2).reshape(B, H, T, -1), state


def _causal_conv(u, prev, w, b):
    T = u.shape[1]
    full = jnp.concatenate([prev.astype(u.dtype), u], axis=1)
    y = sum(full[:, j:j + T] * w[j] for j in range(CONV_W)) + b
    return jax.nn.silu(y), full[:, -(CONV_W - 1):]


def _layer(x, c, past, w_ada, b_ada, g_norm1, w_in, g_q, g_k, w_conv, b_conv, b_i, b_f, g_h,
           w_out, g_norm2, w_ff1, w_ff2):
    B, T, _ = x.shape
    mod = jnp.dot(jax.nn.silu(c), w_ada) + b_ada
    sh1, sc1, gt1, sh2, sc2, gt2 = jnp.split(mod[:, None, :], 6, axis=-1)
    h = _rmsnorm(x, g_norm1) * (1 + sc1) + sh1
    qa, ka, va, qb_raw, kb_raw, vb, ob, ib, fb = _split_cols(jnp.dot(h, w_in))

    qa = _rmsnorm(qa.reshape(B, T, N_HEADS_A, HEAD_DIM_A), g_q).transpose(0, 2, 1, 3)
    ka = _rmsnorm(ka.reshape(B, T, N_HEADS_A, HEAD_DIM_A), g_k).transpose(0, 2, 1, 3)
    va = va.reshape(B, T, N_HEADS_A, HEAD_DIM_A).transpose(0, 2, 1, 3)
    if past is None:
        P = 0
        k_all, v_all = ka, va
        conv_prev = jnp.zeros((B, CONV_W - 1, 2 * D_B), x.dtype)
        state0 = (jnp.zeros((B, N_HEADS_B, HEAD_DIM_B, HEAD_DIM_B), jnp.float32),
                  jnp.zeros((B, N_HEADS_B, HEAD_DIM_B), jnp.float32),
                  jnp.zeros((B, N_HEADS_B), jnp.float32))
    else:
        k_past, v_past, C0, n0, m0, conv_prev = past
        P = k_past.shape[2]
        k_all = jnp.concatenate([k_past.astype(ka.dtype), ka], axis=2)
        v_all = jnp.concatenate([v_past.astype(va.dtype), va], axis=2)
        state0 = (C0.astype(jnp.float32), n0.astype(jnp.float32), m0.astype(jnp.float32))
    q_pos = P + jnp.arange(T, dtype=jnp.int32)
    k_pos = jnp.arange(P + T, dtype=jnp.int32)
    o_a = _sb_attention(qa, k_all, v_all, q_pos, k_pos).transpose(0, 2, 1, 3).reshape(B, T, D_A)

    qk, conv_new = _causal_conv(jnp.concatenate([qb_raw, kb_raw], axis=-1), conv_prev, w_conv, b_conv)
    qb, kb = jnp.split(qk, 2, axis=-1)
    to_h = lambda t: t.reshape(B, T, N_HEADS_B, HEAD_DIM_B).transpose(0, 2, 1, 3).astype(jnp.float32)
    qb, kb, vb4 = to_h(qb), to_h(kb) * (HEAD_DIM_B ** -0.5), to_h(vb)
    ig = (ib + b_i).astype(jnp.float32).transpose(0, 2, 1)
    lf = jax.nn.log_sigmoid((fb + b_f).astype(jnp.float32)).transpose(0, 2, 1)
    hb, (C_new, n_new, m_new) = _mlstm(qb, kb, vb4, ig, lf, state0)
    hb = _rmsnorm(hb, g_h[:, None, :]).transpose(0, 2, 1, 3).reshape(B, T, D_B)
    o_b = hb.astype(x.dtype) * jax.nn.sigmoid(ob)

    y = jnp.dot(jnp.concatenate([o_a, o_b], axis=-1), w_out)
    x = x + gt1 * y
    h2 = _rmsnorm(x, g_norm2) * (1 + sc2) + sh2
    x = x + gt2 * jnp.dot(jnp.square(jax.nn.relu(jnp.dot(h2, w_ff1))), w_ff2)
    return x, (ka, va, C_new, n_new, m_new, conv_new)


def setup_inputs(seed: int = 0) -> dict:
    key = jax.random.key(seed)
    ks = jax.random.split(key, 32)
    nrm = lambda k, shape, s: jax.random.normal(k, shape, jnp.float32) * s
    D = D_MODEL
    return {
        'x_prompt': nrm(ks[0], (BATCH, SEQ, D), 1.0),
        'x_sample': nrm(ks[1], (DEC_BATCH, DEC_SEQ, D), 1.0),
        'c_prompt': nrm(ks[2], (BATCH, D), 1.0),
        'c_sample': nrm(ks[3], (DEC_BATCH, D), 1.0),
        'cache_k': nrm(ks[4], (DEPTH, DEC_BATCH, N_HEADS_A, PAST_LEN, HEAD_DIM_A), 1.0),
        'cache_v': nrm(ks[5], (DEPTH, DEC_BATCH, N_HEADS_A, PAST_LEN, HEAD_DIM_A), 1.0),
        'state_C': nrm(ks[6], (DEPTH, DEC_BATCH, N_HEADS_B, HEAD_DIM_B, HEAD_DIM_B), 0.5),
        'state_n': nrm(ks[7], (DEPTH, DEC_BATCH, N_HEADS_B, HEAD_DIM_B), 0.5),
        'state_m': nrm(ks[8], (DEPTH, DEC_BATCH, N_HEADS_B), 0.5),
        'state_conv': nrm(ks[9], (DEPTH, DEC_BATCH, CONV_W - 1, 2 * D_B), 1.0),
        'w_ada': nrm(ks[10], (DEPTH, D, 6 * D), 0.5 * D ** -0.5),
        'b_ada': nrm(ks[11], (DEPTH, 6 * D), 0.02),
        'g_norm1': 1.0 + nrm(ks[12], (DEPTH, D), 0.02),
        'w_in': nrm(ks[13], (DEPTH, D, D_IN), D ** -0.5),
        'g_q': 1.0 + nrm(ks[14], (DEPTH, HEAD_DIM_A), 0.02),
        'g_k': 1.0 + nrm(ks[15], (DEPTH, HEAD_DIM_A), 0.02),
        'w_conv': nrm(ks[16], (DEPTH, CONV_W, 2 * D_B), CONV_W ** -0.5),
        'b_conv': nrm(ks[17], (DEPTH, 2 * D_B), 0.02),
        'b_i': nrm(ks[18], (DEPTH, N_HEADS_B), 0.1),
        'b_f': jnp.broadcast_to(jnp.linspace(3.0, 6.0, N_HEADS_B), (DEPTH, N_HEADS_B)) + nrm(ks[19], (DEPTH, N_HEADS_B), 0.01),
        'g_h': 1.0 + nrm(ks[20], (DEPTH, N_HEADS_B, HEAD_DIM_B), 0.02),
        'w_out': nrm(ks[21], (DEPTH, D, D), D ** -0.5),
        'g_norm2': 1.0 + nrm(ks[22], (DEPTH, D), 0.02),
        'w_ff1': nrm(ks[23], (DEPTH, D, D_FF), D ** -0.5),
        'w_ff2': nrm(ks[24], (DEPTH, D_FF, D), D_FF ** -0.5),
    }


def _stack(states, i):
    return jnp.stack([s[i] for s in states], axis=0)


def reference(x_prompt, x_sample, c_prompt, c_sample, cache_k, cache_v, state_C, state_n, state_m,
              state_conv, w_ada, b_ada, g_norm1, w_in, g_q, g_k, w_conv, b_conv, b_i, b_f, g_h,
              w_out, g_norm2, w_ff1, w_ff2):
    y_prompt, y_sample = x_prompt, x_sample
    new_p, new_s = [], []
    for l in range(DEPTH):
        wl = (w_ada[l], b_ada[l], g_norm1[l], w_in[l], g_q[l], g_k[l], w_conv[l], b_conv[l],
              b_i[l], b_f[l], g_h[l], w_out[l], g_norm2[l], w_ff1[l], w_ff2[l])
        y_prompt, sp = _layer(y_prompt, c_prompt, None, *wl)
        past = (cache_k[l], cache_v[l], state_C[l], state_n[l], state_m[l], state_conv[l])
        y_sample, ss = _layer(y_sample, c_sample, past, *wl)
        new_p.append(sp)
        new_s.append(ss)
    k_prompt, v_prompt = _stack(new_p, 0), _stack(new_p, 1)
    C_prompt, n_prompt, m_prompt, conv_prompt = _stack(new_p, 2), _stack(new_p, 3), _stack(new_p, 4), _stack(new_p, 5)
    k_sample, v_sample = _stack(new_s, 0), _stack(new_s, 1)
    C_sample, n_sample, m_sample, conv_sample = _stack(new_s, 2), _stack(new_s, 3), _stack(new_s, 4), _stack(new_s, 5)
    return (y_prompt, y_sample, k_prompt, v_prompt, C_prompt, n_prompt, m_prompt, conv_prompt,
            k_sample, v_sample, C_sample, n_sample, m_sample, conv_sample)
```

```python
import functools

import jax
import jax.numpy as jnp
from jax import lax
from jax.experimental import pallas as pl
from jax.experimental.pallas import tpu as pltpu

F32, BF16 = jnp.float32, jnp.bfloat16
EPS = 1e-6
HEAD_DIM_A = 128
N_HEADS_B = 4
CONV_W = 4
LANES = 128
SUBLANES = 8
V7X_VMEM_BYTES = 64 * 2 ** 20
VMEM_LIMIT = 56 * 2 ** 20

ROW_TILE = 512
ATTN_BLOCK = 256
SAMPLE_KEY_BLOCK = 512
MLSTM_CHUNK = 256
FF_TILE = 1024
ADA_TILE = 1024


def _params(semantics, vmem_limit=VMEM_LIMIT):
    return pltpu.CompilerParams(dimension_semantics=semantics, vmem_limit_bytes=vmem_limit)


def _resident(shape, index_map):
    return pl.BlockSpec(shape, index_map, pipeline_mode=pl.Buffered(1))


def _sigmoid(x):
    return 1.0 / (1.0 + jnp.exp(-x))


def _log_sigmoid(x):
    return jnp.minimum(x, 0.0) - jnp.log1p(jnp.exp(-jnp.abs(x)))


def _dot(a, b):
    return jnp.dot(a, b, preferred_element_type=F32)


def _dot_nt(a, b):
    return lax.dot_general(a, b, (((1,), (1,)), ((), ())), preferred_element_type=F32)


def _dot_tn(a, b):
    return lax.dot_general(a, b, (((0,), (0,)), ((), ())), preferred_element_type=F32)


def _split_bf16(x, parts):
    out = []
    for _ in range(parts - 1):
        p = x.astype(BF16)
        out.append(p)
        x = x - p.astype(F32)
    out.append(x.astype(BF16))
    return out


def _modulated_norm(x, g, sc, sh):
    y = x * lax.rsqrt(jnp.mean(x * x, axis=-1, keepdims=True) + EPS)
    return (y * g) * (1.0 + sc) + sh


def _row_blocks(bn, t, rows):
    if t >= rows:
        return 1, rows
    return min(bn, rows // t), t


def _ada_kernel(c_ref, w_ref, b_ref, o_ref):
    c = c_ref[...]
    s = (c * _sigmoid(c)).astype(BF16)
    o_ref[...] = _dot(s, w_ref[...].astype(BF16)) + b_ref[...]


def _ada(c, w, b):
    m, d = c.shape
    n = w.shape[1]
    return pl.pallas_call(
        _ada_kernel,
        out_shape=jax.ShapeDtypeStruct((m, n), F32),
        grid=(n // ADA_TILE,),
        in_specs=[pl.BlockSpec((m, d), lambda j: (0, 0)),
                  pl.BlockSpec((d, ADA_TILE), lambda j: (0, j)),
                  pl.BlockSpec((1, ADA_TILE), lambda j: (0, j))],
        out_specs=pl.BlockSpec((m, ADA_TILE), lambda j: (0, j)),
        compiler_params=_params(("parallel",)),
        name="ada",
    )(c, w, b)


def _proj_attn_kernel(x_ref, sh_ref, sc_ref, g1_ref, w_ref, gq_ref, gk_ref, q_ref, k_ref, v_ref):
    bb, tt, d = x_ref.shape
    nh = q_ref.shape[1]
    da = nh * HEAD_DIM_A
    hb = _modulated_norm(x_ref[...], g1_ref[...], sc_ref[...], sh_ref[...]).reshape(bb * tt, d).astype(BF16)
    for part, (out_ref, g_ref) in enumerate(((q_ref, gq_ref), (k_ref, gk_ref), (v_ref, None))):
        u = _dot(hb, w_ref[:, part * da:(part + 1) * da])
        for hd in range(nh):
            uh = u[:, hd * HEAD_DIM_A:(hd + 1) * HEAD_DIM_A]
            if g_ref is not None:
                uh = uh * lax.rsqrt(jnp.mean(uh * uh, axis=-1, keepdims=True) + EPS) * g_ref[...]
            out_ref[:, hd] = uh.reshape(bb, tt, HEAD_DIM_A).astype(out_ref.dtype)


def _proj_attn(x, mod, g1, w, gq, gk):
    bn, t, d = x.shape
    nh = w.shape[1] // (3 * HEAD_DIM_A)
    bb, tt = _row_blocks(bn, t, ROW_TILE)
    head_spec = pl.BlockSpec((bb, nh, tt, HEAD_DIM_A), lambda b, i: (b, 0, i, 0))
    return pl.pallas_call(
        _proj_attn_kernel,
        out_shape=(jax.ShapeDtypeStruct((bn, nh, t, HEAD_DIM_A), BF16),
                   jax.ShapeDtypeStruct((bn, nh, t, HEAD_DIM_A), F32),
                   jax.ShapeDtypeStruct((bn, nh, t, HEAD_DIM_A), F32)),
        grid=(bn // bb, t // tt),
        in_specs=[pl.BlockSpec((bb, tt, d), lambda b, i: (b, i, 0)),
                  pl.BlockSpec((bb, 1, d), lambda b, i: (b, 0, 0)),
                  pl.BlockSpec((bb, 1, d), lambda b, i: (b, 0, 1)),
                  pl.BlockSpec((1, d), lambda b, i: (0, 0)),
                  _resident(w.shape, lambda b, i: (0, 0)),
                  pl.BlockSpec((1, HEAD_DIM_A), lambda b, i: (0, 0)),
                  pl.BlockSpec((1, HEAD_DIM_A), lambda b, i: (0, 0))],
        out_specs=(head_spec, head_spec, head_spec),
        compiler_params=_params(("parallel", "parallel")),
        name="proj_attn",
    )(x, mod, mod, g1, w, gq, gk)


def _proj_mlstm_kernel(x_ref, sh_ref, sc_ref, g1_ref, w_ref, wg_ref, wgt_ref, u_ref, gc_ref, gr_ref):
    bb, tt, d = x_ref.shape
    hb = _modulated_norm(x_ref[...], g1_ref[...], sc_ref[...], sh_ref[...]).reshape(bb * tt, d).astype(BF16)
    n = w_ref.shape[1]
    step = 1024
    for c in range(n // step):
        u_ref[:, :, c * step:(c + 1) * step] = _dot(hb, w_ref[:, c * step:(c + 1) * step]).reshape(bb, tt, step)
    gc_ref[...] = _dot(hb, wg_ref[...]).reshape(bb, tt, LANES)
    for b in range(bb):
        gr_ref[b] = _dot_nt(wgt_ref[...], hb[b * tt:(b + 1) * tt])


def _proj_mlstm(x, mod, g1, w, wg, wgt):
    bn, t, d = x.shape
    n = w.shape[1]
    bb, tt = _row_blocks(bn, t, ROW_TILE // 2)
    return pl.pallas_call(
        _proj_mlstm_kernel,
        out_shape=(jax.ShapeDtypeStruct((bn, t, n), F32),
                   jax.ShapeDtypeStruct((bn, t, LANES), F32),
                   jax.ShapeDtypeStruct((bn, wgt.shape[0], t), F32)),
        grid=(bn // bb, t // tt),
        in_specs=[pl.BlockSpec((bb, tt, d), lambda b, i: (b, i, 0)),
                  pl.BlockSpec((bb, 1, d), lambda b, i: (b, 0, 0)),
                  pl.BlockSpec((bb, 1, d), lambda b, i: (b, 0, 1)),
                  pl.BlockSpec((1, d), lambda b, i: (0, 0)),
                  _resident(w.shape, lambda b, i: (0, 0)),
                  _resident(wg.shape, lambda b, i: (0, 0)),
                  _resident(wgt.shape, lambda b, i: (0, 0))],
        out_specs=(pl.BlockSpec((bb, tt, n), lambda b, i: (b, i, 0)),
                   pl.BlockSpec((bb, tt, LANES), lambda b, i: (b, i, 0)),
                   pl.BlockSpec((bb, wgt.shape[0], tt), lambda b, i: (b, 0, i))),
        compiler_params=_params(("parallel", "parallel")),
        name="proj_mlstm",
    )(x, mod, mod, g1, w, wg, wgt)


def _conv_kernel(u_ref, tail_ref, prev_ref, w_ref, b_ref, qk_ref, new_ref, full_scr):
    i = pl.program_id(1)
    tt, c = u_ref.shape[1], u_ref.shape[2]
    full_scr[0:SUBLANES] = jnp.where(i == 0, prev_ref[0], tail_ref[0])
    full_scr[SUBLANES:SUBLANES + tt] = u_ref[0]
    first = SUBLANES - (CONV_W - 1)
    y = full_scr[pl.ds(first, tt), :] * w_ref[0:1, :]
    for j in range(1, CONV_W):
        y = y + full_scr[pl.ds(first + j, tt), :] * w_ref[j:j + 1, :]
    y = y + b_ref[...]
    y = y * _sigmoid(y)
    col = lax.broadcasted_iota(jnp.int32, (1, c), 1)
    k_scale = (c // (2 * N_HEADS_B)) ** -0.5
    qk_ref[0] = (y * jnp.where(col >= c // 2, k_scale, 1.0)).astype(qk_ref.dtype)

    @pl.when(i == pl.num_programs(1) - 1)
    def _():
        new_ref[0] = full_scr[SUBLANES + tt - (CONV_W - 1):SUBLANES + tt]


def _conv(u, prev8, w, b):
    bn, t, _ = u.shape
    c = w.shape[1]
    tt = min(t, ATTN_BLOCK)
    sub = tt // SUBLANES
    return pl.pallas_call(
        _conv_kernel,
        out_shape=(jax.ShapeDtypeStruct((bn, t, c), BF16),
                   jax.ShapeDtypeStruct((bn, CONV_W - 1, c), F32)),
        grid=(bn, t // tt),
        in_specs=[pl.BlockSpec((1, tt, c), lambda b, i: (b, i, 0)),
                  pl.BlockSpec((1, SUBLANES, c), lambda b, i: (b, jnp.maximum(i * sub - 1, 0), 0)),
                  pl.BlockSpec((1, SUBLANES, c), lambda b, i: (b, 0, 0)),
                  pl.BlockSpec((CONV_W, c), lambda b, i: (0, 0)),
                  pl.BlockSpec((1, c), lambda b, i: (0, 0))],
        out_specs=(pl.BlockSpec((1, tt, c), lambda b, i: (b, i, 0)),
                   pl.BlockSpec((1, CONV_W - 1, c), lambda b, i: (b, 0, 0))),
        scratch_shapes=[pltpu.VMEM((SUBLANES + tt, c), F32)],
        compiler_params=_params(("parallel", "arbitrary")),
        name="conv",
    )(u, u, prev8, w, b)


def _strict_upper_ones(n):
    r = lax.broadcasted_iota(jnp.int32, (n, n), 0)
    c = lax.broadcasted_iota(jnp.int32, (n, n), 1)
    return (r > c).astype(BF16)


def _sb_block(q, kb, vb, tri, carry, mask):
    z = _dot_nt(q, kb) * (HEAD_DIM_A ** -0.5)
    log_beta = _log_sigmoid(z)
    log_1m = log_beta - z
    if mask is not None:
        log_1m = jnp.where(mask, log_1m, 0.0)
    hi, lo = _split_bf16(log_1m, 2)
    tail = _dot(hi, tri) + _dot(lo, tri)
    if carry is not None:
        tail = tail + carry
    a = jnp.exp(log_beta + tail)
    if mask is not None:
        a = jnp.where(mask, a, 0.0)
    return _dot(a.astype(BF16), vb), jnp.sum(log_1m, axis=1, keepdims=True)


def _attn_kernel(q_ref, kd_ref, vd_ref, kp_ref, vp_ref, o_ref, *, tk, past_is_prefix):
    tq = q_ref.shape[2]
    q = q_ref[0, 0]
    row = lax.broadcasted_iota(jnp.int32, (tq, tq), 0)
    col = lax.broadcasted_iota(jnp.int32, (tq, tq), 1)
    acc, s = _sb_block(q, kd_ref[0, 0].astype(BF16), vd_ref[0, 0].astype(BF16),
                       _strict_upper_ones(tq), None, col < row)
    n_past = pl.program_id(2) * (tq // tk) if past_is_prefix else kp_ref.shape[2] // tk
    tri = _strict_upper_ones(tk)

    def body(jj, carry):
        acc, s = carry
        start = pl.multiple_of((n_past - 1 - jj) * tk, tk)
        kb = kp_ref[0, 0, pl.ds(start, tk), :].astype(BF16)
        vb = vp_ref[0, 0, pl.ds(start, tk), :].astype(BF16)
        pv, ds = _sb_block(q, kb, vb, tri, s, None)
        return acc + pv, s + ds

    acc, s = lax.fori_loop(0, n_past, body, (acc, s))
    o_ref[0] = acc.astype(o_ref.dtype)


def _attn(q, k, v, k_past, v_past):
    bn, nh, t, dh = q.shape
    prefix = k_past is None
    if prefix:
        k_past, v_past = k, v
        tq = tk = min(t, ATTN_BLOCK)
    else:
        tq, tk = t, min(k_past.shape[2], SAMPLE_KEY_BLOCK)
    p = k_past.shape[2]
    blk = pl.BlockSpec((1, 1, tq, dh), lambda b, h, i: (b, h, i, 0))
    past = pl.BlockSpec((1, 1, p, dh), lambda b, h, i: (b, h, 0, 0))
    return pl.pallas_call(
        functools.partial(_attn_kernel, tk=tk, past_is_prefix=prefix),
        out_shape=jax.ShapeDtypeStruct((bn, t, nh * dh), BF16),
        grid=(bn, nh, t // tq),
        in_specs=[blk, blk, blk, past, past],
        out_specs=pl.BlockSpec((1, tq, dh), lambda b, h, i: (b, i, h)),
        compiler_params=_params(("parallel", "parallel", "parallel")),
        name="attn",
    )(q, k, v, k_past, v_past)


def _mlstm_kernel(qk_ref, vo_ref, gc_ref, gr_ref, bc_ref, br_ref, gh_ref, c0_ref, n0_ref, m0_ref,
                  o_ref, c_out_ref, n_out_ref, m_out_ref, c_scr, n_scr, m_scr):
    ci = pl.program_id(1)
    ln = qk_ref.shape[1]
    nh = N_HEADS_B
    dh = qk_ref.shape[2] // (2 * nh)

    @pl.when(ci == 0)
    def _():
        c_scr[...] = c0_ref[0]
        n_scr[...] = n0_ref[0]
        m_scr[...] = m0_ref[0]

    gc = gc_ref[0] + bc_ref[...]
    gr = gr_ref[0] + br_ref[...]
    lfc = _log_sigmoid(gc)
    lfr = _log_sigmoid(gr)
    row = lax.broadcasted_iota(jnp.int32, (ln, ln), 0)
    col = lax.broadcasted_iota(jnp.int32, (ln, ln), 1)
    causal = col <= row
    ones_causal = causal.astype(BF16)
    ones_causal_t = (row <= col).astype(BF16)
    b_rows = sum(_dot(p, ones_causal_t) for p in _split_bf16(lfr, 3))

    for hd in range(nh):
        q = qk_ref[0, :, hd * dh:(hd + 1) * dh]
        k = qk_ref[0, :, (nh + hd) * dh:(nh + hd + 1) * dh]
        v = vo_ref[0, :, hd * dh:(hd + 1) * dh].astype(BF16)
        og = vo_ref[0, :, (nh + hd) * dh:(nh + hd + 1) * dh]
        ig_col = gc[:, hd:hd + 1]
        ig_row = gr[hd:hd + 1, :]
        b_row = b_rows[nh + hd:nh + hd + 1, :]
        lf_b = jnp.broadcast_to(lfc[:, nh + hd:nh + hd + 1], (ln, ln))
        bt = sum(_dot(ones_causal, p) for p in _split_bf16(lf_b, 3))
        b_col = bt[:, :1]
        m_prev = m_scr[hd][:, :1]
        c_prev = c_scr[hd]
        n_prev = n_scr[hd]

        dmat = jnp.where(causal, (bt - b_row) + ig_row, -jnp.inf)
        inter = b_col + m_prev
        m_t = jnp.maximum(inter, jnp.max(dmat, axis=1, keepdims=True))
        w_inter = jnp.exp(inter - m_t)
        s = jnp.exp(dmat - m_t) * _dot_nt(q, k)
        num = w_inter * _dot(q, c_prev.astype(BF16)) + _dot(s.astype(BF16), v)
        den = (w_inter * jnp.sum(q.astype(F32) * n_prev, axis=1, keepdims=True)
               + jnp.sum(s, axis=1, keepdims=True))
        h = num / jnp.maximum(jnp.abs(den), jnp.exp(-m_t))
        hn = h * lax.rsqrt(jnp.mean(h * h, axis=1, keepdims=True) + EPS) * gh_ref[hd]
        o_ref[0, :, hd * dh:(hd + 1) * dh] = (hn * _sigmoid(og)).astype(o_ref.dtype)

        b_last = b_col[ln - 1:ln]
        m_new = m_t[ln - 1:ln]
        w_state = jnp.exp((b_last - b_col) + ig_col - m_new)
        decay = jnp.exp(b_last + m_prev - m_new)
        ks = k.astype(F32) * w_state
        c_scr[hd] = decay * c_prev + _dot_tn(ks.astype(BF16), v)
        n_scr[hd] = decay * n_prev + jnp.sum(ks, axis=0, keepdims=True)
        m_scr[hd] = jnp.broadcast_to(m_new, (1, LANES))

    @pl.when(ci == pl.num_programs(1) - 1)
    def _():
        c_out_ref[0] = c_scr[...]
        n_out_ref[0] = n_scr[...]
        m_out_ref[0] = m_scr[...]


def _mlstm(qk, u, gc, gr, bias_c, bias_r, gh, c0, n0, m0):
    bn, t, c2 = qk.shape
    nh = N_HEADS_B
    dh = c2 // (2 * nh)
    ln = min(t, MLSTM_CHUNK)
    rows = gr.shape[1]
    state = lambda shape: pl.BlockSpec((1,) + shape, lambda b, i: (b, 0, 0, 0))
    return pl.pallas_call(
        _mlstm_kernel,
        out_shape=(jax.ShapeDtypeStruct((bn, t, nh * dh), BF16),
                   jax.ShapeDtypeStruct((bn, nh, dh, dh), F32),
                   jax.ShapeDtypeStruct((bn, nh, 1, dh), F32),
                   jax.ShapeDtypeStruct((bn, nh, 1, LANES), F32)),
        grid=(bn, t // ln),
        in_specs=[pl.BlockSpec((1, ln, c2), lambda b, i: (b, i, 0)),
                  pl.BlockSpec((1, ln, c2), lambda b, i: (b, i, 1)),
                  pl.BlockSpec((1, ln, LANES), lambda b, i: (b, i, 0)),
                  pl.BlockSpec((1, rows, ln), lambda b, i: (b, 0, i)),
                  pl.BlockSpec((1, LANES), lambda b, i: (0, 0)),
                  pl.BlockSpec((rows, 1), lambda b, i: (0, 0)),
                  pl.BlockSpec((nh, 1, dh), lambda b, i: (0, 0, 0)),
                  state((nh, dh, dh)), state((nh, 1, dh)), state((nh, 1, LANES))],
        out_specs=(pl.BlockSpec((1, ln, nh * dh), lambda b, i: (b, i, 0)),
                   state((nh, dh, dh)), state((nh, 1, dh)), state((nh, 1, LANES))),
        scratch_shapes=[pltpu.VMEM((nh, dh, dh), F32), pltpu.VMEM((nh, 1, dh), F32),
                        pltpu.VMEM((nh, 1, LANES), F32)],
        compiler_params=_params(("parallel", "arbitrary")),
        name="mlstm",
    )(qk, u, gc, gr, bias_c, bias_r, gh, c0, n0, m0)


def _out_proj_kernel(x_ref, oa_ref, ob_ref, gt_ref, w_ref, o_ref):
    bb, tt, d = x_ref.shape
    da, db = oa_ref.shape[2], ob_ref.shape[2]
    y = (_dot(oa_ref[...].reshape(bb * tt, da), w_ref[0:da])
         + _dot(ob_ref[...].reshape(bb * tt, db), w_ref[da:da + db]))
    o_ref[...] = x_ref[...] + gt_ref[...] * y.reshape(bb, tt, d)


def _out_proj(x, oa, ob, mod, w):
    bn, t, d = x.shape
    bb, tt = _row_blocks(bn, t, ROW_TILE)
    rows = lambda width: pl.BlockSpec((bb, tt, width), lambda b, i: (b, i, 0))
    return pl.pallas_call(
        _out_proj_kernel,
        out_shape=jax.ShapeDtypeStruct((bn, t, d), F32),
        grid=(bn // bb, t // tt),
        in_specs=[rows(d), rows(oa.shape[2]), rows(ob.shape[2]),
                  pl.BlockSpec((bb, 1, d), lambda b, i: (b, 0, 2)),
                  _resident(w.shape, lambda b, i: (0, 0))],
        out_specs=rows(d),
        compiler_params=_params(("parallel", "parallel")),
        name="out_proj",
    )(x, oa, ob, mod, w)


def _ffn_kernel(x_ref, sh_ref, sc_ref, gt_ref, g2_ref, w1_ref, w2_ref, o_ref, h_scr, acc_scr):
    f = pl.program_id(2)
    bb, tt, d = x_ref.shape

    @pl.when(f == 0)
    def _():
        h = _modulated_norm(x_ref[...], g2_ref[...], sc_ref[...], sh_ref[...])
        h_scr[...] = h.reshape(bb * tt, d).astype(BF16)
        acc_scr[...] = jnp.zeros_like(acc_scr)

    a = jnp.maximum(_dot(h_scr[...], w1_ref[...]), 0.0)
    acc_scr[...] += _dot((a * a).astype(BF16), w2_ref[...])

    @pl.when(f == pl.num_programs(2) - 1)
    def _():
        o_ref[...] = x_ref[...] + gt_ref[...] * acc_scr[...].reshape(bb, tt, d)


def _ffn(x, mod, g2, w1, w2):
    bn, t, d = x.shape
    dff = w1.shape[1]
    bb, tt = _row_blocks(bn, t, ROW_TILE)
    rows = pl.BlockSpec((bb, tt, d), lambda b, i, f: (b, i, 0))
    chunk = lambda c: pl.BlockSpec((bb, 1, d), lambda b, i, f: (b, 0, c))
    return pl.pallas_call(
        _ffn_kernel,
        out_shape=jax.ShapeDtypeStruct((bn, t, d), F32),
        grid=(bn // bb, t // tt, dff // FF_TILE),
        in_specs=[rows, chunk(3), chunk(4), chunk(5),
                  pl.BlockSpec((1, d), lambda b, i, f: (0, 0)),
                  pl.BlockSpec((d, FF_TILE), lambda b, i, f: (0, f)),
                  pl.BlockSpec((FF_TILE, d), lambda b, i, f: (f, 0))],
        out_specs=rows,
        scratch_shapes=[pltpu.VMEM((bb * tt, d), BF16), pltpu.VMEM((bb * tt, d), F32)],
        compiler_params=_params(("parallel", "parallel", "arbitrary")),
        name="ffn",
    )(x, mod, mod, mod, g2, w1, w2)


def _layer(x, mod, past, wts):
    bn, t, d = x.shape
    da = wts["w_in_a"].shape[1] // 3
    db = wts["w_in_b"].shape[1] // 4
    nh, dh = N_HEADS_B, db // N_HEADS_B
    if past is None:
        k_past = v_past = None
        conv_prev = jnp.zeros((bn, CONV_W - 1, 2 * db), F32)
        c0 = jnp.zeros((bn, nh, dh, dh), F32)
        n0 = jnp.zeros((bn, nh, dh), F32)
        m0 = jnp.zeros((bn, nh), F32)
    else:
        k_past, v_past, c0, n0, m0, conv_prev = (a.astype(F32) for a in past)

    q, k, v = _proj_attn(x, mod, wts["g_norm1"], wts["w_in_a"], wts["g_q"], wts["g_k"])
    u, gc, gr = _proj_mlstm(x, mod, wts["g_norm1"], wts["w_in_b"], wts["w_gate"], wts["w_gate_t"])
    o_a = _attn(q, k, v, k_past, v_past)

    prev8 = jnp.pad(conv_prev, ((0, 0), (SUBLANES - (CONV_W - 1), 0), (0, 0)))
    qk, conv_new = _conv(u, prev8, wts["w_conv"], wts["b_conv"])
    o_b, c_new, n_new, m_new = _mlstm(
        qk, u, gc, gr, wts["gate_bias_c"], wts["gate_bias_r"], wts["g_h"], c0, n0.reshape(bn, nh, 1, dh),
        jnp.broadcast_to(m0[:, :, None, None], (bn, nh, 1, LANES)))

    x = _out_proj(x, o_a, o_b, mod, wts["w_out"])
    x = _ffn(x, mod, wts["g_norm2"], wts["w_ff1"], wts["w_ff2"])
    return x, (k, v, c_new, n_new.reshape(bn, nh, dh), m_new[:, :, 0, 0], conv_new)


def _layer_weights(l, w_in, g_norm1, g_q, g_k, w_conv, b_conv, b_i, b_f, g_h, w_out, g_norm2, w_ff1, w_ff2):
    d = w_in.shape[1]
    nh = N_HEADS_B
    da = 3 * (d // 2)
    n_main = w_in.shape[2] - 2 * nh
    w_gate = w_in[l][:, n_main:]
    bias = jnp.concatenate([b_i[l], b_f[l]])
    rows = 2 * SUBLANES
    return {
        "w_in_a": w_in[l][:, :da].astype(BF16),
        "w_in_b": w_in[l][:, da:n_main].astype(BF16),
        "w_gate": jnp.pad(w_gate, ((0, 0), (0, LANES - 2 * nh))).astype(BF16),
        "w_gate_t": jnp.pad(w_gate.T, ((0, rows - 2 * nh), (0, 0))).astype(BF16),
        "gate_bias_c": jnp.pad(bias, (0, LANES - 2 * nh)).reshape(1, LANES),
        "gate_bias_r": jnp.pad(bias, (0, rows - 2 * nh)).reshape(rows, 1),
        "g_norm1": g_norm1[l].reshape(1, d), "g_norm2": g_norm2[l].reshape(1, d),
        "g_q": g_q[l].reshape(1, -1), "g_k": g_k[l].reshape(1, -1),
        "w_conv": w_conv[l], "b_conv": b_conv[l].reshape(1, -1),
        "g_h": g_h[l].reshape(nh, 1, -1),
        "w_out": w_out[l].astype(BF16), "w_ff1": w_ff1[l].astype(BF16), "w_ff2": w_ff2[l].astype(BF16),
    }


def kernel(x_prompt, x_sample, c_prompt, c_sample, cache_k, cache_v, state_C, state_n, state_m, state_conv,
           w_ada, b_ada, g_norm1, w_in, g_q, g_k, w_conv, b_conv, b_i, b_f, g_h, w_out, g_norm2, w_ff1, w_ff2):
    depth = w_ada.shape[0]
    bp, bs = c_prompt.shape[0], c_sample.shape[0]
    c_rows = -(-(bp + bs) // (2 * SUBLANES)) * (2 * SUBLANES)
    c_all = jnp.pad(jnp.concatenate([c_prompt, c_sample], axis=0), ((0, c_rows - bp - bs), (0, 0)))
    y_prompt, y_sample = x_prompt, x_sample
    new_p, new_s = [], []
    for l in range(depth):
        wts = _layer_weights(l, w_in, g_norm1, g_q, g_k, w_conv, b_conv, b_i, b_f, g_h, w_out, g_norm2,
                             w_ff1, w_ff2)
        mod = _ada(c_all, w_ada[l], b_ada[l].reshape(1, -1))
        mod_p = mod[:bp].reshape(bp, 1, -1)
        mod_s = mod[bp:bp + bs].reshape(bs, 1, -1)
        y_prompt, sp = _layer(y_prompt, mod_p, None, wts)
        past = (cache_k[l], cache_v[l], state_C[l], state_n[l], state_m[l], state_conv[l])
        y_sample, ss = _layer(y_sample, mod_s, past, wts)
        new_p.append(sp)
        new_s.append(ss)
    stack = lambda states, i: jnp.stack([s[i] for s in states], axis=0)
    return ((y_prompt, y_sample) + tuple(stack(new_p, i) for i in range(6))
            + tuple(stack(new_s, i) for i in range(6)))
```

```python
import functools

import jax
import jax.numpy as jnp
from jax import lax
from jax.experimental import pallas as pl
from jax.experimental.pallas import tpu as pltpu

F32, BF16 = jnp.float32, jnp.bfloat16
EPS = 1e-6
HEAD_DIM_A = 128
N_HEADS_B = 4
CONV_W = 4
LANES = 128
SUBLANES = 8
F32_EXP_UNDERFLOW = 104.0
NO_PAST = 1e30
V7X_VMEM_BYTES = 64 * 2 ** 20
VMEM_LIMIT = 56 * 2 ** 20

ROW_TILE = 512
ATTN_BLOCK = 256
SAMPLE_KEY_BLOCK = 256
MLSTM_CHUNK = 256
FF_TILE = 1024
ADA_TILE = 1024
PROJ_COLS = 256


def _params(semantics, vmem_limit=VMEM_LIMIT):
    return pltpu.CompilerParams(dimension_semantics=semantics, vmem_limit_bytes=vmem_limit)


def _resident(shape, index_map):
    return pl.BlockSpec(shape, index_map, pipeline_mode=pl.Buffered(1))


def _sigmoid(x):
    return 1.0 / (1.0 + jnp.exp(-x))


def _log_sigmoid(x):
    return jnp.minimum(x, 0.0) - jnp.log1p(jnp.exp(-jnp.abs(x)))


def _dot(a, b):
    return jnp.dot(a, b, preferred_element_type=F32)


def _dot_nt(a, b):
    return lax.dot_general(a, b, (((1,), (1,)), ((), ())), preferred_element_type=F32)


def _dot_tn(a, b):
    return lax.dot_general(a, b, (((0,), (0,)), ((), ())), preferred_element_type=F32)


def _split_bf16(x, parts):
    out = []
    for _ in range(parts - 1):
        p = x.astype(BF16)
        out.append(p)
        x = x - p.astype(F32)
    out.append(x.astype(BF16))
    return out


def _modulated_norm(x, g, sc, sh):
    y = x * lax.rsqrt(jnp.mean(x * x, axis=-1, keepdims=True) + EPS)
    return (y * g) * (1.0 + sc) + sh


def _row_blocks(bn, t, rows):
    if t >= rows:
        return 1, rows
    return min(bn, rows // t), t


def _ada_kernel(c_ref, w_ref, b_ref, o_ref):
    c = c_ref[...]
    s = (c * _sigmoid(c)).astype(BF16)
    o_ref[...] = _dot(s, w_ref[...].astype(BF16)) + b_ref[...]


def _ada(c, w, b):
    m, d = c.shape
    n = w.shape[1]
    return pl.pallas_call(
        _ada_kernel,
        out_shape=jax.ShapeDtypeStruct((m, n), F32),
        grid=(n // ADA_TILE,),
        in_specs=[pl.BlockSpec((m, d), lambda j: (0, 0)),
                  pl.BlockSpec((d, ADA_TILE), lambda j: (0, j)),
                  pl.BlockSpec((1, ADA_TILE), lambda j: (0, j))],
        out_specs=pl.BlockSpec((m, ADA_TILE), lambda j: (0, j)),
        compiler_params=_params(("parallel",)),
        name="ada",
    )(c, w, b)


def _proj_attn_kernel(x_ref, sh_ref, sc_ref, g1_ref, w_ref, gq_ref, gk_ref, q_ref, k_ref, v_ref):
    bb, tt, d = x_ref.shape
    nh = q_ref.shape[1]
    da = nh * HEAD_DIM_A
    hb = _modulated_norm(x_ref[...], g1_ref[...], sc_ref[...], sh_ref[...]).reshape(bb * tt, d).astype(BF16)
    for part, (out_ref, g_ref) in enumerate(((q_ref, gq_ref), (k_ref, gk_ref), (v_ref, None))):
        u = _dot(hb, w_ref[:, part * da:(part + 1) * da])
        for hd in range(nh):
            uh = u[:, hd * HEAD_DIM_A:(hd + 1) * HEAD_DIM_A]
            if g_ref is not None:
                uh = uh * lax.rsqrt(jnp.mean(uh * uh, axis=-1, keepdims=True) + EPS) * g_ref[...]
            out_ref[:, hd] = uh.reshape(bb, tt, HEAD_DIM_A).astype(out_ref.dtype)


def _proj_attn(x, mod, g1, w, gq, gk):
    bn, t, d = x.shape
    nh = w.shape[1] // (3 * HEAD_DIM_A)
    bb, tt = _row_blocks(bn, t, ROW_TILE)
    head_spec = pl.BlockSpec((bb, nh, tt, HEAD_DIM_A), lambda b, i: (b, 0, i, 0))
    return pl.pallas_call(
        _proj_attn_kernel,
        out_shape=(jax.ShapeDtypeStruct((bn, nh, t, HEAD_DIM_A), BF16),
                   jax.ShapeDtypeStruct((bn, nh, t, HEAD_DIM_A), F32),
                   jax.ShapeDtypeStruct((bn, nh, t, HEAD_DIM_A), F32)),
        grid=(bn // bb, t // tt),
        in_specs=[pl.BlockSpec((bb, tt, d), lambda b, i: (b, i, 0)),
                  pl.BlockSpec((bb, 1, d), lambda b, i: (b, 0, 0)),
                  pl.BlockSpec((bb, 1, d), lambda b, i: (b, 0, 1)),
                  pl.BlockSpec((1, d), lambda b, i: (0, 0)),
                  _resident(w.shape, lambda b, i: (0, 0)),
                  pl.BlockSpec((1, HEAD_DIM_A), lambda b, i: (0, 0)),
                  pl.BlockSpec((1, HEAD_DIM_A), lambda b, i: (0, 0))],
        out_specs=(head_spec, head_spec, head_spec),
        compiler_params=_params(("parallel", "parallel")),
        name="proj_attn",
    )(x, mod, mod, g1, w, gq, gk)


def _proj_mlstm_kernel(x_ref, sh_ref, sc_ref, g1_ref, w_ref, wg_ref, wgt_ref, prev_ref, wc_ref, bc_ref,
                       qk_ref, vo_ref, gc_ref, gr_ref, new_ref, full_scr):
    i = pl.program_id(1)
    bb, tt, d = x_ref.shape
    c = qk_ref.shape[2]
    hb = _modulated_norm(x_ref[...], g1_ref[...], sc_ref[...], sh_ref[...]).reshape(bb * tt, d).astype(BF16)

    @pl.when(i == 0)
    def _():
        full_scr[:, 0:SUBLANES] = prev_ref[...]

    @pl.when(i > 0)
    def _():
        full_scr[:, 0:SUBLANES] = full_scr[:, tt:tt + SUBLANES]

    step = min(PROJ_COLS, c // 2)
    first = SUBLANES - (CONV_W - 1)
    k_scale = (c // (2 * N_HEADS_B)) ** -0.5
    for lo in range(0, c, step):
        cols = slice(lo, lo + step)
        full_scr[:, SUBLANES:SUBLANES + tt, cols] = _dot(hb, w_ref[:, cols]).reshape(bb, tt, step)
    for lo in range(0, c, step):
        cols = slice(lo, lo + step)
        vo_ref[:, :, cols] = _dot(hb, w_ref[:, c + lo:c + lo + step]).reshape(bb, tt, step).astype(vo_ref.dtype)
        y = full_scr[:, pl.ds(first, tt), cols] * wc_ref[0:1, cols]
        for j in range(1, CONV_W):
            y = y + full_scr[:, pl.ds(first + j, tt), cols] * wc_ref[j:j + 1, cols]
        y = y + bc_ref[:, cols]
        y = y * _sigmoid(y)
        qk_ref[:, :, cols] = (y * k_scale if lo >= c // 2 else y).astype(qk_ref.dtype)
    gc_ref[...] = _dot(hb, wg_ref[...]).reshape(bb, tt, LANES)
    for b in range(bb):
        gr_ref[b] = _dot_nt(wgt_ref[...], hb[b * tt:(b + 1) * tt])

    @pl.when(i == pl.num_programs(1) - 1)
    def _():
        new_ref[...] = full_scr[:, SUBLANES + tt - (CONV_W - 1):SUBLANES + tt]


def _proj_mlstm(x, mod, g1, w, wg, wgt, prev8, wc, bc):
    bn, t, d = x.shape
    c = w.shape[1] // 2
    bb, tt = _row_blocks(bn, t, ROW_TILE)
    const = lambda a: pl.BlockSpec(a.shape, lambda b, i: (0,) * a.ndim)
    return pl.pallas_call(
        _proj_mlstm_kernel,
        out_shape=(jax.ShapeDtypeStruct((bn, t, c), BF16),
                   jax.ShapeDtypeStruct((bn, t, c), BF16),
                   jax.ShapeDtypeStruct((bn, t, LANES), F32),
                   jax.ShapeDtypeStruct((bn, wgt.shape[0], t), F32),
                   jax.ShapeDtypeStruct((bn, CONV_W - 1, c), F32)),
        grid=(bn // bb, t // tt),
        in_specs=[pl.BlockSpec((bb, tt, d), lambda b, i: (b, i, 0)),
                  pl.BlockSpec((bb, 1, d), lambda b, i: (b, 0, 0)),
                  pl.BlockSpec((bb, 1, d), lambda b, i: (b, 0, 1)),
                  const(g1),
                  _resident(w.shape, lambda b, i: (0, 0)),
                  _resident(wg.shape, lambda b, i: (0, 0)),
                  _resident(wgt.shape, lambda b, i: (0, 0)),
                  pl.BlockSpec((bb, SUBLANES, c), lambda b, i: (b, 0, 0)),
                  const(wc), const(bc)],
        out_specs=(pl.BlockSpec((bb, tt, c), lambda b, i: (b, i, 0)),
                   pl.BlockSpec((bb, tt, c), lambda b, i: (b, i, 0)),
                   pl.BlockSpec((bb, tt, LANES), lambda b, i: (b, i, 0)),
                   pl.BlockSpec((bb, wgt.shape[0], tt), lambda b, i: (b, 0, i)),
                   pl.BlockSpec((bb, CONV_W - 1, c), lambda b, i: (b, 0, 0))),
        scratch_shapes=[pltpu.VMEM((bb, SUBLANES + tt, c), F32)],
        compiler_params=_params(("parallel", "arbitrary")),
        name="proj_mlstm",
    )(x, mod, mod, g1, w, wg, wgt, prev8, wc, bc)


def _strict_upper_ones(n):
    r = lax.broadcasted_iota(jnp.int32, (n, n), 0)
    c = lax.broadcasted_iota(jnp.int32, (n, n), 1)
    return (r > c).astype(BF16)


def _sb_block(q, kb, vb, tri, carry, mask):
    z = _dot_nt(q, kb) * (HEAD_DIM_A ** -0.5)
    log_beta = _log_sigmoid(z)
    log_1m = log_beta - z
    if mask is not None:
        log_1m = jnp.where(mask, log_1m, 0.0)
    hi, lo = _split_bf16(log_1m, 2)
    tail = _dot(hi, tri) + _dot(lo, tri)
    if carry is not None:
        tail = tail + carry
    a = jnp.exp(log_beta + tail)
    if mask is not None:
        a = jnp.where(mask, a, 0.0)
    return _dot(a.astype(BF16), vb), jnp.sum(log_1m, axis=1, keepdims=True)


def _attn_kernel(q_ref, kd_ref, vd_ref, kp_ref, vp_ref, o_ref, *, tk, past_is_prefix):
    tq = q_ref.shape[2]
    q = q_ref[0, 0]
    row = lax.broadcasted_iota(jnp.int32, (tq, tq), 0)
    col = lax.broadcasted_iota(jnp.int32, (tq, tq), 1)
    n_past = pl.program_id(2) * (tq // tk) if past_is_prefix else kp_ref.shape[2] // tk
    tri = _strict_upper_ones(tk)

    def past_block(jj, carry):
        start = pl.multiple_of(jnp.maximum(n_past - 1 - jj, 0) * tk, tk)
        kb = kp_ref[0, 0, pl.ds(start, tk), :].astype(BF16)
        vb = vp_ref[0, 0, pl.ds(start, tk), :].astype(BF16)
        return _sb_block(q, kb, vb, tri, carry, None)

    acc, s = _sb_block(q, kd_ref[0, 0].astype(BF16), vd_ref[0, 0].astype(BF16),
                       tri if tq == tk else _strict_upper_ones(tq), None, col < row)
    pv, ds = past_block(0, jnp.where(n_past > 0, s, -NO_PAST))
    acc, s = acc + pv, s + ds

    def more(c):
        jj, _, s = c
        return jnp.logical_and(jj < n_past, jnp.max(s) > -F32_EXP_UNDERFLOW)

    def body(c):
        jj, acc, s = c
        pv, ds = past_block(jj, s)
        return jj + 1, acc + pv, s + ds

    _, acc, s = lax.while_loop(more, body, (jnp.int32(1), acc, s))
    o_ref[0] = acc.astype(o_ref.dtype)


def _attn(q, k, v, k_past, v_past):
    bn, nh, t, dh = q.shape
    prefix = k_past is None
    if prefix:
        k_past, v_past = k, v
        tq = tk = min(t, ATTN_BLOCK)
    else:
        tq, tk = t, min(k_past.shape[2], SAMPLE_KEY_BLOCK)
    p = k_past.shape[2]
    blk = pl.BlockSpec((1, 1, tq, dh), lambda b, h, i: (b, h, i, 0))
    past = pl.BlockSpec((1, 1, p, dh), lambda b, h, i: (b, h, 0, 0))
    return pl.pallas_call(
        functools.partial(_attn_kernel, tk=tk, past_is_prefix=prefix),
        out_shape=jax.ShapeDtypeStruct((bn, t, nh * dh), BF16),
        grid=(bn, nh, t // tq),
        in_specs=[blk, blk, blk, past, past],
        out_specs=pl.BlockSpec((1, tq, dh), lambda b, h, i: (b, i, h)),
        compiler_params=_params(("parallel", "parallel", "parallel")),
        name="attn",
    )(q, k, v, k_past, v_past)


def _mlstm_kernel(qk_ref, vo_ref, gc_ref, gr_ref, bc_ref, br_ref, gh_ref, c0_ref, n0_ref, m0_ref,
                  o_ref, c_out_ref, n_out_ref, m_out_ref, c_scr, n_scr, m_scr):
    ci = pl.program_id(1)
    ln = qk_ref.shape[1]
    nh = N_HEADS_B
    dh = qk_ref.shape[2] // (2 * nh)

    @pl.when(ci == 0)
    def _():
        c_scr[...] = c0_ref[0]
        n_scr[...] = n0_ref[0]
        m_scr[...] = m0_ref[0]

    gc = gc_ref[0] + bc_ref[...]
    gr = gr_ref[0] + br_ref[...]
    lfc = _log_sigmoid(gc)
    lfr = _log_sigmoid(gr)
    row = lax.broadcasted_iota(jnp.int32, (ln, ln), 0)
    col = lax.broadcasted_iota(jnp.int32, (ln, ln), 1)
    causal = col <= row
    ones_causal = causal.astype(BF16)
    ones_causal_t = (row <= col).astype(BF16)
    b_rows = sum(_dot(p, ones_causal_t) for p in _split_bf16(lfr, 3))

    for hd in range(nh):
        q = qk_ref[0, :, hd * dh:(hd + 1) * dh]
        k = qk_ref[0, :, (nh + hd) * dh:(nh + hd + 1) * dh]
        v = vo_ref[0, :, hd * dh:(hd + 1) * dh]
        og = vo_ref[0, :, (nh + hd) * dh:(nh + hd + 1) * dh].astype(F32)
        ig_col = gc[:, hd:hd + 1]
        ig_row = gr[hd:hd + 1, :]
        b_row = b_rows[nh + hd:nh + hd + 1, :]
        lf_b = jnp.broadcast_to(lfc[:, nh + hd:nh + hd + 1], (ln, ln))
        bt = sum(_dot(ones_causal, p) for p in _split_bf16(lf_b, 3))
        b_col = bt[:, :1]
        m_prev = m_scr[hd][:, :1]
        c_prev = c_scr[hd]
        n_prev = n_scr[hd]

        dmat = jnp.where(causal, (bt - b_row) + ig_row, -jnp.inf)
        inter = b_col + m_prev
        m_t = jnp.maximum(inter, jnp.max(dmat, axis=1, keepdims=True))
        w_inter = jnp.exp(inter - m_t)
        s = jnp.exp(dmat - m_t) * _dot_nt(q, k)
        num = w_inter * _dot(q, c_prev.astype(BF16)) + _dot(s.astype(BF16), v)
        den = (w_inter * jnp.sum(q.astype(F32) * n_prev, axis=1, keepdims=True)
               + jnp.sum(s, axis=1, keepdims=True))
        h = num / jnp.maximum(jnp.abs(den), jnp.exp(-m_t))
        hn = h * lax.rsqrt(jnp.mean(h * h, axis=1, keepdims=True) + EPS) * gh_ref[hd]
        o_ref[0, :, hd * dh:(hd + 1) * dh] = (hn * _sigmoid(og)).astype(o_ref.dtype)

        b_last = b_col[ln - 1:ln]
        m_new = m_t[ln - 1:ln]
        w_state = jnp.exp((b_last - b_col) + ig_col - m_new)
        decay = jnp.exp(b_last + m_prev - m_new)
        ks = k.astype(F32) * w_state
        c_scr[hd] = decay * c_prev + _dot_tn(ks.astype(BF16), v)
        n_scr[hd] = decay * n_prev + jnp.sum(ks, axis=0, keepdims=True)
        m_scr[hd] = jnp.broadcast_to(m_new, (1, LANES))

    @pl.when(ci == pl.num_programs(1) - 1)
    def _():
        c_out_ref[0] = c_scr[...]
        n_out_ref[0] = n_scr[...]
        m_out_ref[0] = m_scr[...]


def _mlstm(qk, vo, gc, gr, bias_c, bias_r, gh, c0, n0, m0):
    bn, t, c2 = qk.shape
    nh = N_HEADS_B
    dh = c2 // (2 * nh)
    ln = min(t, MLSTM_CHUNK)
    rows = gr.shape[1]
    state = lambda shape: pl.BlockSpec((1,) + shape, lambda b, i: (b, 0, 0, 0))
    return pl.pallas_call(
        _mlstm_kernel,
        out_shape=(jax.ShapeDtypeStruct((bn, t, nh * dh), BF16),
                   jax.ShapeDtypeStruct((bn, nh, dh, dh), F32),
                   jax.ShapeDtypeStruct((bn, nh, 1, dh), F32),
                   jax.ShapeDtypeStruct((bn, nh, 1, LANES), F32)),
        grid=(bn, t // ln),
        in_specs=[pl.BlockSpec((1, ln, c2), lambda b, i: (b, i, 0)),
                  pl.BlockSpec((1, ln, c2), lambda b, i: (b, i, 0)),
                  pl.BlockSpec((1, ln, LANES), lambda b, i: (b, i, 0)),
                  pl.BlockSpec((1, rows, ln), lambda b, i: (b, 0, i)),
                  pl.BlockSpec((1, LANES), lambda b, i: (0, 0)),
                  pl.BlockSpec((rows, 1), lambda b, i: (0, 0)),
                  pl.BlockSpec((nh, 1, dh), lambda b, i: (0, 0, 0)),
                  state((nh, dh, dh)), state((nh, 1, dh)), state((nh, 1, LANES))],
        out_specs=(pl.BlockSpec((1, ln, nh * dh), lambda b, i: (b, i, 0)),
                   state((nh, dh, dh)), state((nh, 1, dh)), state((nh, 1, LANES))),
        scratch_shapes=[pltpu.VMEM((nh, dh, dh), F32), pltpu.VMEM((nh, 1, dh), F32),
                        pltpu.VMEM((nh, 1, LANES), F32)],
        compiler_params=_params(("parallel", "arbitrary")),
        name="mlstm",
    )(qk, vo, gc, gr, bias_c, bias_r, gh, c0, n0, m0)


def _out_proj_kernel(x_ref, oa_ref, ob_ref, gt_ref, w_ref, o_ref):
    bb, tt, d = x_ref.shape
    da, db = oa_ref.shape[2], ob_ref.shape[2]
    y = (_dot(oa_ref[...].reshape(bb * tt, da), w_ref[0:da])
         + _dot(ob_ref[...].reshape(bb * tt, db), w_ref[da:da + db]))
    o_ref[...] = x_ref[...] + gt_ref[...] * y.reshape(bb, tt, d)


def _out_proj(x, oa, ob, mod, w):
    bn, t, d = x.shape
    bb, tt = _row_blocks(bn, t, ROW_TILE)
    rows = lambda width: pl.BlockSpec((bb, tt, width), lambda b, i: (b, i, 0))
    return pl.pallas_call(
        _out_proj_kernel,
        out_shape=jax.ShapeDtypeStruct((bn, t, d), F32),
        grid=(bn // bb, t // tt),
        in_specs=[rows(d), rows(oa.shape[2]), rows(ob.shape[2]),
                  pl.BlockSpec((bb, 1, d), lambda b, i: (b, 0, 2)),
                  _resident(w.shape, lambda b, i: (0, 0))],
        out_specs=rows(d),
        compiler_params=_params(("parallel", "parallel")),
        name="out_proj",
    )(x, oa, ob, mod, w)


def _ffn_kernel(x_ref, sh_ref, sc_ref, gt_ref, g2_ref, w1_ref, w2_ref, o_ref, h_scr, acc_scr):
    f = pl.program_id(2)
    bb, tt, d = x_ref.shape

    @pl.when(f == 0)
    def _():
        h = _modulated_norm(x_ref[...], g2_ref[...], sc_ref[...], sh_ref[...])
        h_scr[...] = h.reshape(bb * tt, d).astype(BF16)
        acc_scr[...] = jnp.zeros_like(acc_scr)

    a = jnp.maximum(_dot(h_scr[...], w1_ref[...]), 0.0)
    acc_scr[...] += _dot((a * a).astype(BF16), w2_ref[...])

    @pl.when(f == pl.num_programs(2) - 1)
    def _():
        o_ref[...] = x_ref[...] + gt_ref[...] * acc_scr[...].reshape(bb, tt, d)


def _ffn(x, mod, g2, w1, w2):
    bn, t, d = x.shape
    dff = w1.shape[1]
    bb, tt = _row_blocks(bn, t, ROW_TILE)
    rows = pl.BlockSpec((bb, tt, d), lambda b, i, f: (b, i, 0))
    chunk = lambda c: pl.BlockSpec((bb, 1, d), lambda b, i, f: (b, 0, c))
    return pl.pallas_call(
        _ffn_kernel,
        out_shape=jax.ShapeDtypeStruct((bn, t, d), F32),
        grid=(bn // bb, t // tt, dff // FF_TILE),
        in_specs=[rows, chunk(3), chunk(4), chunk(5),
                  pl.BlockSpec((1, d), lambda b, i, f: (0, 0)),
                  pl.BlockSpec((d, FF_TILE), lambda b, i, f: (0, f)),
                  pl.BlockSpec((FF_TILE, d), lambda b, i, f: (f, 0))],
        out_specs=rows,
        scratch_shapes=[pltpu.VMEM((bb * tt, d), BF16), pltpu.VMEM((bb * tt, d), F32)],
        compiler_params=_params(("parallel", "parallel", "arbitrary")),
        name="ffn",
    )(x, mod, mod, mod, g2, w1, w2)


def _layer(x, mod, past, wts):
    bn, t, d = x.shape
    da = wts["w_in_a"].shape[1] // 3
    db = wts["w_in_b"].shape[1] // 4
    nh, dh = N_HEADS_B, db // N_HEADS_B
    if past is None:
        k_past = v_past = None
        conv_prev = jnp.zeros((bn, CONV_W - 1, 2 * db), F32)
        c0 = jnp.zeros((bn, nh, dh, dh), F32)
        n0 = jnp.zeros((bn, nh, dh), F32)
        m0 = jnp.zeros((bn, nh), F32)
    else:
        k_past, v_past, c0, n0, m0, conv_prev = (a.astype(F32) for a in past)

    q, k, v = _proj_attn(x, mod, wts["g_norm1"], wts["w_in_a"], wts["g_q"], wts["g_k"])
    prev8 = jnp.pad(conv_prev, ((0, 0), (SUBLANES - (CONV_W - 1), 0), (0, 0)))
    qk, vo, gc, gr, conv_new = _proj_mlstm(x, mod, wts["g_norm1"], wts["w_in_b"], wts["w_gate"], wts["w_gate_t"],
                                           prev8, wts["w_conv"], wts["b_conv"])
    o_a = _attn(q, k, v, k_past, v_past)
    o_b, c_new, n_new, m_new = _mlstm(
        qk, vo, gc, gr, wts["gate_bias_c"], wts["gate_bias_r"], wts["g_h"], c0, n0.reshape(bn, nh, 1, dh),
        jnp.broadcast_to(m0[:, :, None, None], (bn, nh, 1, LANES)))

    x = _out_proj(x, o_a, o_b, mod, wts["w_out"])
    x = _ffn(x, mod, wts["g_norm2"], wts["w_ff1"], wts["w_ff2"])
    return x, (k, v, c_new, n_new.reshape(bn, nh, dh), m_new[:, :, 0, 0], conv_new)


def _layer_weights(l, w_in, g_norm1, g_q, g_k, w_conv, b_conv, b_i, b_f, g_h, w_out, g_norm2, w_ff1, w_ff2):
    d = w_in.shape[1]
    nh = N_HEADS_B
    da = 3 * (d // 2)
    n_main = w_in.shape[2] - 2 * nh
    w_gate = w_in[l][:, n_main:]
    bias = jnp.concatenate([b_i[l], b_f[l]])
    rows = 2 * SUBLANES
    return {
        "w_in_a": w_in[l][:, :da].astype(BF16),
        "w_in_b": w_in[l][:, da:n_main].astype(BF16),
        "w_gate": jnp.pad(w_gate, ((0, 0), (0, LANES - 2 * nh))).astype(BF16),
        "w_gate_t": jnp.pad(w_gate.T, ((0, rows - 2 * nh), (0, 0))).astype(BF16),
        "gate_bias_c": jnp.pad(bias, (0, LANES - 2 * nh)).reshape(1, LANES),
        "gate_bias_r": jnp.pad(bias, (0, rows - 2 * nh)).reshape(rows, 1),
        "g_norm1": g_norm1[l].reshape(1, d), "g_norm2": g_norm2[l].reshape(1, d),
        "g_q": g_q[l].reshape(1, -1), "g_k": g_k[l].reshape(1, -1),
        "w_conv": w_conv[l], "b_conv": b_conv[l].reshape(1, -1),
        "g_h": g_h[l].reshape(nh, 1, -1),
        "w_out": w_out[l].astype(BF16), "w_ff1": w_ff1[l].astype(BF16), "w_ff2": w_ff2[l].astype(BF16),
    }


def kernel(x_prompt, x_sample, c_prompt, c_sample, cache_k, cache_v, state_C, state_n, state_m, state_conv,
           w_ada, b_ada, g_norm1, w_in, g_q, g_k, w_conv, b_conv, b_i, b_f, g_h, w_out, g_norm2, w_ff1, w_ff2):
    depth = w_ada.shape[0]
    bp, bs = c_prompt.shape[0], c_sample.shape[0]
    c_rows = -(-(bp + bs) // (2 * SUBLANES)) * (2 * SUBLANES)
    c_all = jnp.pad(jnp.concatenate([c_prompt, c_sample], axis=0), ((0, c_rows - bp - bs), (0, 0)))
    y_prompt, y_sample = x_prompt, x_sample
    new_p, new_s = [], []
    for l in range(depth):
        wts = _layer_weights(l, w_in, g_norm1, g_q, g_k, w_conv, b_conv, b_i, b_f, g_h, w_out, g_norm2,
                             w_ff1, w_ff2)
        mod = _ada(c_all, w_ada[l], b_ada[l].reshape(1, -1))
        mod_p = mod[:bp].reshape(bp, 1, -1)
        mod_s = mod[bp:bp + bs].reshape(bs, 1, -1)
        y_prompt, sp = _layer(y_prompt, mod_p, None, wts)
        past = (cache_k[l], cache_v[l], state_C[l], state_n[l], state_m[l], state_conv[l])
        y_sample, ss = _layer(y_sample, mod_s, past, wts)
        new_p.append(sp)
        new_s.append(ss)
    stack = lambda states, i: jnp.stack([s[i] for s in states], axis=0)
    return ((y_prompt, y_sample) + tuple(stack(new_p, i) for i in range(6))
            + tuple(stack(new_s, i) for i in range(6)))
```

```python
import functools

import jax
import jax.numpy as jnp
from jax import lax
from jax.experimental import pallas as pl
from jax.experimental.pallas import tpu as pltpu

F32, BF16 = jnp.float32, jnp.bfloat16
EPS = 1e-6
HEAD_DIM_A = 128
N_HEADS_B = 4
CONV_W = 4
LANES = 128
SUBLANES = 8
F32_EXP2_UNDERFLOW = 151.0
LOG2_E = 1.4426950408889634
NO_PAST = 1e30
V7X_VMEM_BYTES = 64 * 2 ** 20
VMEM_LIMIT = 56 * 2 ** 20

ROW_TILE = 512
ATTN_BLOCK = 256
SAMPLE_KEY_BLOCK = 256
MLSTM_CHUNK = 256
FF_TILE = 1024
ADA_TILE = 1024
ATTN_HEADS_PER_STEP = 4
SUFFIX_SUM_TERMS = 1
PROJ_COLS = 256


def _params(semantics, vmem_limit=VMEM_LIMIT):
    return pltpu.CompilerParams(dimension_semantics=semantics, vmem_limit_bytes=vmem_limit)


def _resident(shape, index_map):
    return pl.BlockSpec(shape, index_map, pipeline_mode=pl.Buffered(1))


def _sigmoid(x):
    return 1.0 / (1.0 + jnp.exp(-x))


def _log_sigmoid(x):
    return jnp.minimum(x, 0.0) - jnp.log1p(jnp.exp(-jnp.abs(x)))


def _dot(a, b):
    return jnp.dot(a, b, preferred_element_type=F32)


def _dot_nt(a, b):
    return lax.dot_general(a, b, (((1,), (1,)), ((), ())), preferred_element_type=F32)


def _dot_tn(a, b):
    return lax.dot_general(a, b, (((0,), (0,)), ((), ())), preferred_element_type=F32)


def _split_bf16(x, parts):
    out = []
    for _ in range(parts - 1):
        p = x.astype(BF16)
        out.append(p)
        x = x - p.astype(F32)
    out.append(x.astype(BF16))
    return out


def _modulated_norm(x, g, sc, sh):
    y = x * lax.rsqrt(jnp.mean(x * x, axis=-1, keepdims=True) + EPS)
    return (y * g) * (1.0 + sc) + sh


def _row_blocks(bn, t, rows):
    if t >= rows:
        return 1, rows
    return min(bn, rows // t), t


def _ada_kernel(c_ref, w_ref, b_ref, o_ref):
    c = c_ref[...]
    s = (c * _sigmoid(c)).astype(BF16)
    o_ref[...] = _dot(s, w_ref[...].astype(BF16)) + b_ref[...]


def _ada(c, w, b):
    m, d = c.shape
    n = w.shape[1]
    return pl.pallas_call(
        _ada_kernel,
        out_shape=jax.ShapeDtypeStruct((m, n), F32),
        grid=(n // ADA_TILE,),
        in_specs=[pl.BlockSpec((m, d), lambda j: (0, 0)),
                  pl.BlockSpec((d, ADA_TILE), lambda j: (0, j)),
                  pl.BlockSpec((1, ADA_TILE), lambda j: (0, j))],
        out_specs=pl.BlockSpec((m, ADA_TILE), lambda j: (0, j)),
        compiler_params=_params(("parallel",)),
        name="ada",
    )(c, w, b)


def _proj_attn_kernel(x_ref, sh_ref, sc_ref, g1_ref, w_ref, gq_ref, gk_ref, q_ref, k_ref, v_ref, kb_ref, vb_ref):
    bb, tt, d = x_ref.shape
    nh = q_ref.shape[1]
    da = nh * HEAD_DIM_A
    hb = _modulated_norm(x_ref[...], g1_ref[...], sc_ref[...], sh_ref[...]).reshape(bb * tt, d).astype(BF16)
    outs = (((q_ref,), gq_ref), ((k_ref, kb_ref), gk_ref), ((v_ref, vb_ref), None))
    for part, (out_refs, g_ref) in enumerate(outs):
        u = _dot(hb, w_ref[:, part * da:(part + 1) * da])
        for hd in range(nh):
            uh = u[:, hd * HEAD_DIM_A:(hd + 1) * HEAD_DIM_A]
            if g_ref is not None:
                uh = uh * lax.rsqrt(jnp.mean(uh * uh, axis=-1, keepdims=True) + EPS) * g_ref[...]
            for out_ref in out_refs:
                out_ref[:, hd] = uh.reshape(bb, tt, HEAD_DIM_A).astype(out_ref.dtype)


def _proj_attn(x, mod, g1, w, gq, gk):
    bn, t, d = x.shape
    nh = w.shape[1] // (3 * HEAD_DIM_A)
    bb, tt = _row_blocks(bn, t, ROW_TILE)
    head_spec = pl.BlockSpec((bb, nh, tt, HEAD_DIM_A), lambda b, i: (b, 0, i, 0))
    return pl.pallas_call(
        _proj_attn_kernel,
        out_shape=tuple(jax.ShapeDtypeStruct((bn, nh, t, HEAD_DIM_A), dt) for dt in (BF16, F32, F32, BF16, BF16)),
        grid=(bn // bb, t // tt),
        in_specs=[pl.BlockSpec((bb, tt, d), lambda b, i: (b, i, 0)),
                  pl.BlockSpec((bb, 1, d), lambda b, i: (b, 0, 0)),
                  pl.BlockSpec((bb, 1, d), lambda b, i: (b, 0, 1)),
                  pl.BlockSpec((1, d), lambda b, i: (0, 0)),
                  _resident(w.shape, lambda b, i: (0, 0)),
                  pl.BlockSpec((1, HEAD_DIM_A), lambda b, i: (0, 0)),
                  pl.BlockSpec((1, HEAD_DIM_A), lambda b, i: (0, 0))],
        out_specs=(head_spec,) * 5,
        compiler_params=_params(("parallel", "parallel")),
        name="proj_attn",
    )(x, mod, mod, g1, w, gq, gk)


def _proj_mlstm_kernel(x_ref, sh_ref, sc_ref, g1_ref, w_ref, wg_ref, wgt_ref, prev_ref, wc_ref, bc_ref,
                       qk_ref, vo_ref, gc_ref, gr_ref, new_ref, full_scr):
    i = pl.program_id(1)
    bb, tt, d = x_ref.shape
    c = qk_ref.shape[2]
    hb = _modulated_norm(x_ref[...], g1_ref[...], sc_ref[...], sh_ref[...]).reshape(bb * tt, d).astype(BF16)

    @pl.when(i == 0)
    def _():
        full_scr[:, 0:SUBLANES] = prev_ref[...]

    @pl.when(i > 0)
    def _():
        full_scr[:, 0:SUBLANES] = full_scr[:, tt:tt + SUBLANES]

    step = min(PROJ_COLS, c // 2)
    first = SUBLANES - (CONV_W - 1)
    k_scale = (c // (2 * N_HEADS_B)) ** -0.5
    for lo in range(0, c, step):
        cols = slice(lo, lo + step)
        full_scr[:, SUBLANES:SUBLANES + tt, cols] = _dot(hb, w_ref[:, cols]).reshape(bb, tt, step)
    for lo in range(0, c, step):
        cols = slice(lo, lo + step)
        vo_ref[:, :, cols] = _dot(hb, w_ref[:, c + lo:c + lo + step]).reshape(bb, tt, step).astype(vo_ref.dtype)
        y = full_scr[:, pl.ds(first, tt), cols] * wc_ref[0:1, cols]
        for j in range(1, CONV_W):
            y = y + full_scr[:, pl.ds(first + j, tt), cols] * wc_ref[j:j + 1, cols]
        y = y + bc_ref[:, cols]
        y = y * _sigmoid(y)
        qk_ref[:, :, cols] = (y * k_scale if lo >= c // 2 else y).astype(qk_ref.dtype)
    gc_ref[...] = _dot(hb, wg_ref[...]).reshape(bb, tt, LANES)
    for b in range(bb):
        gr_ref[b] = _dot_nt(wgt_ref[...], hb[b * tt:(b + 1) * tt])

    @pl.when(i == pl.num_programs(1) - 1)
    def _():
        new_ref[...] = full_scr[:, SUBLANES + tt - (CONV_W - 1):SUBLANES + tt]


def _proj_mlstm(x, mod, g1, w, wg, wgt, prev8, wc, bc):
    bn, t, d = x.shape
    c = w.shape[1] // 2
    bb, tt = _row_blocks(bn, t, ROW_TILE)
    const = lambda a: pl.BlockSpec(a.shape, lambda b, i: (0,) * a.ndim)
    return pl.pallas_call(
        _proj_mlstm_kernel,
        out_shape=(jax.ShapeDtypeStruct((bn, t, c), BF16),
                   jax.ShapeDtypeStruct((bn, t, c), BF16),
                   jax.ShapeDtypeStruct((bn, t, LANES), F32),
                   jax.ShapeDtypeStruct((bn, wgt.shape[0], t), F32),
                   jax.ShapeDtypeStruct((bn, CONV_W - 1, c), F32)),
        grid=(bn // bb, t // tt),
        in_specs=[pl.BlockSpec((bb, tt, d), lambda b, i: (b, i, 0)),
                  pl.BlockSpec((bb, 1, d), lambda b, i: (b, 0, 0)),
                  pl.BlockSpec((bb, 1, d), lambda b, i: (b, 0, 1)),
                  const(g1),
                  _resident(w.shape, lambda b, i: (0, 0)),
                  _resident(wg.shape, lambda b, i: (0, 0)),
                  _resident(wgt.shape, lambda b, i: (0, 0)),
                  pl.BlockSpec((bb, SUBLANES, c), lambda b, i: (b, 0, 0)),
                  const(wc), const(bc)],
        out_specs=(pl.BlockSpec((bb, tt, c), lambda b, i: (b, i, 0)),
                   pl.BlockSpec((bb, tt, c), lambda b, i: (b, i, 0)),
                   pl.BlockSpec((bb, tt, LANES), lambda b, i: (b, i, 0)),
                   pl.BlockSpec((bb, wgt.shape[0], tt), lambda b, i: (b, 0, i)),
                   pl.BlockSpec((bb, CONV_W - 1, c), lambda b, i: (b, 0, 0))),
        scratch_shapes=[pltpu.VMEM((bb, SUBLANES + tt, c), F32)],
        compiler_params=_params(("parallel", "arbitrary")),
        name="proj_mlstm",
    )(x, mod, mod, g1, w, wg, wgt, prev8, wc, bc)


def _strict_upper_ones(n):
    r = lax.broadcasted_iota(jnp.int32, (n, n), 0)
    c = lax.broadcasted_iota(jnp.int32, (n, n), 1)
    return (r > c).astype(BF16)


def _sb_block(q, kb, vb, tri, carry, mask):
    z = _dot_nt(q, kb) * (HEAD_DIM_A ** -0.5 * LOG2_E)
    log_beta = jnp.minimum(z, 0.0) - jnp.log2(1.0 + jnp.exp2(-jnp.abs(z)))
    log_1m = log_beta - z
    if mask is not None:
        log_1m = jnp.where(mask, log_1m, 0.0)
    tail = sum(_dot(p, tri) for p in _split_bf16(log_1m, SUFFIX_SUM_TERMS))
    total = tail[:, :1] + log_1m[:, :1]
    if carry is not None:
        tail = tail + carry
    a = jnp.exp2(log_beta + tail)
    if mask is not None:
        a = jnp.where(mask, a, 0.0)
    return _dot(a.astype(BF16), vb), total


def _attn_kernel(q_ref, kd_ref, vd_ref, kp_ref, vp_ref, o_ref, *, tk, past_is_prefix):
    hps, tq, dh = q_ref.shape[1], q_ref.shape[2], q_ref.shape[3]
    row = lax.broadcasted_iota(jnp.int32, (tq, tq), 0)
    col = lax.broadcasted_iota(jnp.int32, (tq, tq), 1)
    n_past = pl.program_id(2) * (tq // tk) if past_is_prefix else kp_ref.shape[2] // tk
    tri = _strict_upper_ones(tk)
    tri_diag = tri if tq == tk else _strict_upper_ones(tq)

    def past_block(h, jj, carry):
        start = pl.multiple_of(jnp.maximum(n_past - 1 - jj, 0) * tk, tk)
        kb = kp_ref[0, h, pl.ds(start, tk), :].astype(BF16)
        vb = vp_ref[0, h, pl.ds(start, tk), :].astype(BF16)
        return _sb_block(q_ref[0, h], kb, vb, tri, carry, None)

    state = []
    for h in range(hps):
        acc, s = _sb_block(q_ref[0, h], kd_ref[0, h].astype(BF16), vd_ref[0, h].astype(BF16), tri_diag, None,
                           col < row)
        pv, ds = past_block(h, 0, jnp.where(n_past > 0, s, -NO_PAST))
        state.append((acc + pv, s + ds))

    for h, (acc, s) in enumerate(state):
        def more(c):
            jj, _, s = c
            return jnp.logical_and(jj < n_past, jnp.max(s) > -F32_EXP2_UNDERFLOW)

        def body(c, h=h):
            jj, acc, s = c
            pv, ds = past_block(h, jj, s)
            return jj + 1, acc + pv, s + ds

        _, acc, s = lax.while_loop(more, body, (jnp.int32(1), acc, s))
        o_ref[0, :, h * dh:(h + 1) * dh] = acc.astype(o_ref.dtype)


def _attn(q, k, v, k_past, v_past):
    bn, nh, t, dh = q.shape
    prefix = k_past is None
    if prefix:
        k_past, v_past = k, v
        tq = tk = min(t, ATTN_BLOCK)
    else:
        tq, tk = t, min(k_past.shape[2], SAMPLE_KEY_BLOCK)
    p = k_past.shape[2]
    hps = min(nh, ATTN_HEADS_PER_STEP)
    blk =pl.BlockSpec((1, hps, tq, dh), lambda b, h, i: (b, h, i, 0))
    past = pl.BlockSpec((1, hps, p, dh), lambda b, h, i: (b, h, 0, 0))
    return pl.pallas_call(
        functools.partial(_attn_kernel, tk=tk, past_is_prefix=prefix),
        out_shape=jax.ShapeDtypeStruct((bn, t, nh * dh), BF16),
        grid=(bn, nh // hps, t // tq),
        in_specs=[blk, blk, blk, past, past],
        out_specs=pl.BlockSpec((1, tq, hps * dh), lambda b, h, i: (b, i, h)),
        compiler_params=_params(("parallel", "parallel", "parallel")),
        name="attn",
    )(q, k, v, k_past, v_past)


def _mlstm_kernel(qk_ref, vo_ref, gc_ref, gr_ref, bc_ref, br_ref, gh_ref, c0_ref, n0_ref, m0_ref,
                  o_ref, c_out_ref, n_out_ref, m_out_ref, c_scr, n_scr, m_scr):
    ci = pl.program_id(1)
    ln = qk_ref.shape[1]
    nh = N_HEADS_B
    dh = qk_ref.shape[2] // (2 * nh)

    @pl.when(ci == 0)
    def _():
        c_scr[...] = c0_ref[0]
        n_scr[...] = n0_ref[0]
        m_scr[...] = m0_ref[0]

    gc = gc_ref[0] + bc_ref[...]
    gr = gr_ref[0] + br_ref[...]
    lfc = _log_sigmoid(gc)
    lfr = _log_sigmoid(gr)
    row = lax.broadcasted_iota(jnp.int32, (ln, ln), 0)
    col = lax.broadcasted_iota(jnp.int32, (ln, ln), 1)
    causal = col <= row
    ones_causal = causal.astype(BF16)
    ones_causal_t = (row <= col).astype(BF16)
    b_rows = sum(_dot(p, ones_causal_t) for p in _split_bf16(lfr, 3))

    for hd in range(nh):
        q = qk_ref[0, :, hd * dh:(hd + 1) * dh]
        k = qk_ref[0, :, (nh + hd) * dh:(nh + hd + 1) * dh]
        v = vo_ref[0, :, hd * dh:(hd + 1) * dh]
        og = vo_ref[0, :, (nh + hd) * dh:(nh + hd + 1) * dh].astype(F32)
        ig_col = gc[:, hd:hd + 1]
        ig_row = gr[hd:hd + 1, :]
        b_row = b_rows[nh + hd:nh + hd + 1, :]
        lf_b = jnp.broadcast_to(lfc[:, nh + hd:nh + hd + 1], (ln, ln))
        bt = sum(_dot(ones_causal, p) for p in _split_bf16(lf_b, 3))
        b_col = bt[:, :1]
        m_prev = m_scr[hd][:, :1]
        c_prev = c_scr[hd]
        n_prev = n_scr[hd]

        dmat = jnp.where(causal, (bt - b_row) + ig_row, -jnp.inf)
        inter = b_col + m_prev
        m_t = jnp.maximum(inter, jnp.max(dmat, axis=1, keepdims=True))
        w_inter = jnp.exp(inter - m_t)
        s = jnp.exp(dmat - m_t) * _dot_nt(q, k)
        num = w_inter * _dot(q, c_prev.astype(BF16)) + _dot(s.astype(BF16), v)
        den = (w_inter * jnp.sum(q.astype(F32) * n_prev, axis=1, keepdims=True)
               + jnp.sum(s, axis=1, keepdims=True))
        h = num / jnp.maximum(jnp.abs(den), jnp.exp(-m_t))
        hn = h * lax.rsqrt(jnp.mean(h * h, axis=1, keepdims=True) + EPS) * gh_ref[hd]
        o_ref[0, :, hd * dh:(hd + 1) * dh] = (hn * _sigmoid(og)).astype(o_ref.dtype)

        b_last = b_col[ln - 1:ln]
        m_new = m_t[ln - 1:ln]
        w_state = jnp.exp((b_last - b_col) + ig_col - m_new)
        decay = jnp.exp(b_last + m_prev - m_new)
        ks = k.astype(F32) * w_state
        c_scr[hd] = decay * c_prev + _dot_tn(ks.astype(BF16), v)
        n_scr[hd] = decay * n_prev + jnp.sum(ks, axis=0, keepdims=True)
        m_scr[hd] = jnp.broadcast_to(m_new, (1, LANES))

    @pl.when(ci == pl.num_programs(1) - 1)
    def _():
        c_out_ref[0] = c_scr[...]
        n_out_ref[0] = n_scr[...]
        m_out_ref[0] = m_scr[...]


def _mlstm(qk, vo, gc, gr, bias_c, bias_r, gh, c0, n0, m0):
    bn, t, c2 = qk.shape
    nh = N_HEADS_B
    dh = c2 // (2 * nh)
    ln = min(t, MLSTM_CHUNK)
    rows = gr.shape[1]
    state = lambda shape: pl.BlockSpec((1,) + shape, lambda b, i: (b, 0, 0, 0))
    return pl.pallas_call(
        _mlstm_kernel,
        out_shape=(jax.ShapeDtypeStruct((bn, t, nh * dh), BF16),
                   jax.ShapeDtypeStruct((bn, nh, dh, dh), F32),
                   jax.ShapeDtypeStruct((bn, nh, 1, dh), F32),
                   jax.ShapeDtypeStruct((bn, nh, 1, LANES), F32)),
        grid=(bn, t // ln),
        in_specs=[pl.BlockSpec((1, ln, c2), lambda b, i: (b, i, 0)),
                  pl.BlockSpec((1, ln, c2), lambda b, i: (b, i, 0)),
                  pl.BlockSpec((1, ln, LANES), lambda b, i: (b, i, 0)),
                  pl.BlockSpec((1, rows, ln), lambda b, i: (b, 0, i)),
                  pl.BlockSpec((1, LANES), lambda b, i: (0, 0)),
                  pl.BlockSpec((rows, 1), lambda b, i: (0, 0)),
                  pl.BlockSpec((nh, 1, dh), lambda b, i: (0, 0, 0)),
                  state((nh, dh, dh)), state((nh, 1, dh)), state((nh, 1, LANES))],
        out_specs=(pl.BlockSpec((1, ln, nh * dh), lambda b, i: (b, i, 0)),
                   state((nh, dh, dh)), state((nh, 1, dh)), state((nh, 1, LANES))),
        scratch_shapes=[pltpu.VMEM((nh, dh, dh), F32), pltpu.VMEM((nh, 1, dh), F32),
                        pltpu.VMEM((nh, 1, LANES), F32)],
        compiler_params=_params(("parallel", "arbitrary")),
        name="mlstm",
    )(qk, vo, gc, gr, bias_c, bias_r, gh, c0, n0, m0)


def _out_proj_kernel(x_ref, oa_ref, ob_ref, gt_ref, w_ref, o_ref):
    bb, tt, d = x_ref.shape
    da, db = oa_ref.shape[2], ob_ref.shape[2]
    y = (_dot(oa_ref[...].reshape(bb * tt, da), w_ref[0:da])
         + _dot(ob_ref[...].reshape(bb * tt, db), w_ref[da:da + db]))
    o_ref[...] = x_ref[...] + gt_ref[...] * y.reshape(bb, tt, d)


def _out_proj(x, oa, ob, mod, w):
    bn, t, d = x.shape
    bb, tt = _row_blocks(bn, t, ROW_TILE)
    rows = lambda width: pl.BlockSpec((bb, tt, width), lambda b, i: (b, i, 0))
    return pl.pallas_call(
        _out_proj_kernel,
        out_shape=jax.ShapeDtypeStruct((bn, t, d), F32),
        grid=(bn // bb, t // tt),
        in_specs=[rows(d), rows(oa.shape[2]), rows(ob.shape[2]),
                  pl.BlockSpec((bb, 1, d), lambda b, i: (b, 0, 2)),
                  _resident(w.shape, lambda b, i: (0, 0))],
        out_specs=rows(d),
        compiler_params=_params(("parallel", "parallel")),
        name="out_proj",
    )(x, oa, ob, mod, w)


def _ffn_kernel(x_ref, sh_ref, sc_ref, gt_ref, g2_ref, w1_ref, w2_ref, o_ref, h_scr, acc_scr):
    f = pl.program_id(2)
    bb, tt, d = x_ref.shape

    @pl.when(f == 0)
    def _():
        h = _modulated_norm(x_ref[...], g2_ref[...], sc_ref[...], sh_ref[...])
        h_scr[...] = h.reshape(bb * tt, d).astype(BF16)
        acc_scr[...] = jnp.zeros_like(acc_scr)

    a = jnp.maximum(_dot(h_scr[...], w1_ref[...]), 0.0)
    acc_scr[...] += _dot((a * a).astype(BF16), w2_ref[...])

    @pl.when(f == pl.num_programs(2) - 1)
    def _():
        o_ref[...] = x_ref[...] + gt_ref[...] * acc_scr[...].reshape(bb, tt, d)


def _ffn(x, mod, g2, w1, w2):
    bn, t, d = x.shape
    dff = w1.shape[1]
    bb, tt = _row_blocks(bn, t, ROW_TILE)
    rows = pl.BlockSpec((bb, tt, d), lambda b, i, f: (b, i, 0))
    chunk = lambda c: pl.BlockSpec((bb, 1, d), lambda b, i, f: (b, 0, c))
    return pl.pallas_call(
        _ffn_kernel,
        out_shape=jax.ShapeDtypeStruct((bn, t, d), F32),
        grid=(bn // bb, t // tt, dff // FF_TILE),
        in_specs=[rows, chunk(3), chunk(4), chunk(5),
                  pl.BlockSpec((1, d), lambda b, i, f: (0, 0)),
                  pl.BlockSpec((d, FF_TILE), lambda b, i, f: (0, f)),
                  pl.BlockSpec((FF_TILE, d), lambda b, i, f: (f, 0))],
        out_specs=rows,
        scratch_shapes=[pltpu.VMEM((bb * tt, d), BF16), pltpu.VMEM((bb * tt, d), F32)],
        compiler_params=_params(("parallel", "parallel", "arbitrary")),
        name="ffn",
    )(x, mod, mod, mod, g2, w1, w2)


def _layer(x, mod, past, wts):
    bn, t, d = x.shape
    da = wts["w_in_a"].shape[1] // 3
    db = wts["w_in_b"].shape[1] // 4
    nh, dh = N_HEADS_B, db // N_HEADS_B
    if past is None:
        k_past = v_past = None
        conv_prev = jnp.zeros((bn, CONV_W - 1, 2 * db), F32)
        c0 = jnp.zeros((bn, nh, dh, dh), F32)
        n0 = jnp.zeros((bn, nh, dh), F32)
        m0 = jnp.zeros((bn, nh), F32)
    else:
        k_past, v_past, c0, n0, m0, conv_prev = (a.astype(F32) for a in past)

    q, k, v, k_bf, v_bf = _proj_attn(x, mod, wts["g_norm1"], wts["w_in_a"], wts["g_q"], wts["g_k"])
    prev8 = jnp.pad(conv_prev, ((0, 0), (SUBLANES - (CONV_W - 1), 0), (0, 0)))
    qk, vo, gc, gr, conv_new = _proj_mlstm(x, mod, wts["g_norm1"], wts["w_in_b"], wts["w_gate"], wts["w_gate_t"],
                                           prev8, wts["w_conv"], wts["b_conv"])
    o_a = _attn(q, k_bf, v_bf, k_past, v_past)
    o_b, c_new, n_new, m_new = _mlstm(
        qk, vo, gc, gr, wts["gate_bias_c"], wts["gate_bias_r"], wts["g_h"], c0, n0.reshape(bn, nh, 1, dh),
        jnp.broadcast_to(m0[:, :, None, None], (bn, nh, 1, LANES)))

    x = _out_proj(x, o_a, o_b, mod, wts["w_out"])
    x = _ffn(x, mod, wts["g_norm2"], wts["w_ff1"], wts["w_ff2"])
    return x, (k, v, c_new, n_new.reshape(bn, nh, dh), m_new[:, :, 0, 0], conv_new)


def _layer_weights(l, w_in, g_norm1, g_q, g_k, w_conv, b_conv, b_i, b_f, g_h, w_out, g_norm2, w_ff1, w_ff2):
    d = w_in.shape[1]
    nh = N_HEADS_B
    da = 3 * (d // 2)
    n_main = w_in.shape[2] - 2 * nh
    w_gate = w_in[l][:, n_main:]
    bias = jnp.concatenate([b_i[l], b_f[l]])
    rows = 2 * SUBLANES
    return {
        "w_in_a": w_in[l][:, :da].astype(BF16),
        "w_in_b": w_in[l][:, da:n_main].astype(BF16),
        "w_gate": jnp.pad(w_gate, ((0, 0), (0, LANES - 2 * nh))).astype(BF16),
        "w_gate_t": jnp.pad(w_gate.T, ((0, rows - 2 * nh), (0, 0))).astype(BF16),
        "gate_bias_c": jnp.pad(bias, (0, LANES - 2 * nh)).reshape(1, LANES),
        "gate_bias_r": jnp.pad(bias, (0, rows - 2 * nh)).reshape(rows, 1),
        "g_norm1": g_norm1[l].reshape(1, d), "g_norm2": g_norm2[l].reshape(1, d),
        "g_q": g_q[l].reshape(1, -1), "g_k": g_k[l].reshape(1, -1),
        "w_conv": w_conv[l], "b_conv": b_conv[l].reshape(1, -1),
        "g_h": g_h[l].reshape(nh, 1, -1),
        "w_out": w_out[l].astype(BF16), "w_ff1": w_ff1[l].astype(BF16), "w_ff2": w_ff2[l].astype(BF16),
    }


def kernel(x_prompt, x_sample, c_prompt, c_sample, cache_k, cache_v, state_C, state_n, state_m, state_conv,
           w_ada, b_ada, g_norm1, w_in, g_q, g_k, w_conv, b_conv, b_i, b_f, g_h, w_out, g_norm2, w_ff1, w_ff2):
    depth = w_ada.shape[0]
    bp, bs = c_prompt.shape[0], c_sample.shape[0]
    c_rows = -(-(bp + bs) // (2 * SUBLANES)) * (2 * SUBLANES)
    c_all = jnp.pad(jnp.concatenate([c_prompt, c_sample], axis=0), ((0, c_rows - bp - bs), (0, 0)))
    y_prompt, y_sample = x_prompt, x_sample
    new_p, new_s = [], []
    for l in range(depth):
        wts = _layer_weights(l, w_in, g_norm1, g_q, g_k, w_conv, b_conv, b_i, b_f, g_h, w_out, g_norm2,
                             w_ff1, w_ff2)
        mod = _ada(c_all, w_ada[l], b_ada[l].reshape(1, -1))
        mod_p = mod[:bp].reshape(bp, 1, -1)
        mod_s = mod[bp:bp + bs].reshape(bs, 1, -1)
        y_prompt, sp = _layer(y_prompt, mod_p, None, wts)
        past = (cache_k[l], cache_v[l], state_C[l], state_n[l], state_m[l], state_conv[l])
        y_sample, ss = _layer(y_sample, mod_s, past, wts)
        new_p.append(sp)
        new_s.append(ss)
    stack = lambda states, i: jnp.stack([s[i] for s in states], axis=0)
    return ((y_prompt, y_sample) + tuple(stack(new_p, i) for i in range(6))
            + tuple(stack(new_s, i) for i in range(6)))
```

```python
import functools

import jax
import jax.numpy as jnp
from jax import lax
from jax.experimental import pallas as pl
from jax.experimental.pallas import tpu as pltpu

F32, BF16 = jnp.float32, jnp.bfloat16
EPS = 1e-6
HEAD_DIM_A = 128
N_HEADS_B = 4
CONV_W = 4
LANES = 128
SUBLANES = 8
F32_EXP2_UNDERFLOW = 151.0
LOG2_E = 1.4426950408889634
NO_PAST = 1e30
V7X_VMEM_BYTES = 64 * 2 ** 20
VMEM_LIMIT = 56 * 2 ** 20

ROW_TILE = 512
ATTN_BLOCK = 256
SAMPLE_KEY_BLOCK = 256
MLSTM_CHUNK = 256
FF_TILE = 1024
ADA_TILE = 1024
ATTN_HEADS_PER_STEP = 4
SUFFIX_SUM_TERMS = 1
PROJ_COLS = 256


def _params(semantics, vmem_limit=VMEM_LIMIT):
    return pltpu.CompilerParams(dimension_semantics=semantics, vmem_limit_bytes=vmem_limit)


def _resident(shape, index_map):
    return pl.BlockSpec(shape, index_map, pipeline_mode=pl.Buffered(1))


def _sigmoid(x):
    return 1.0 / (1.0 + jnp.exp(-x))


def _log_sigmoid(x):
    return jnp.minimum(x, 0.0) - jnp.log1p(jnp.exp(-jnp.abs(x)))


def _dot(a, b):
    return jnp.dot(a, b, preferred_element_type=F32)


def _dot_nt(a, b):
    return lax.dot_general(a, b, (((1,), (1,)), ((), ())), preferred_element_type=F32)


def _dot_tn(a, b):
    return lax.dot_general(a, b, (((0,), (0,)), ((), ())), preferred_element_type=F32)


def _split_bf16(x, parts):
    out = []
    for _ in range(parts - 1):
        p = x.astype(BF16)
        out.append(p)
        x = x - p.astype(F32)
    out.append(x.astype(BF16))
    return out


def _modulated_norm(x, g, sc, sh):
    y = x * lax.rsqrt(jnp.mean(x * x, axis=-1, keepdims=True) + EPS)
    return (y * g) * (1.0 + sc) + sh


def _row_blocks(bn, t, rows):
    if t >= rows:
        return 1, rows
    return min(bn, rows // t), t


def _ada_kernel(c_ref, w_ref, b_ref, o_ref):
    c = c_ref[...]
    s = (c * _sigmoid(c)).astype(BF16)
    o_ref[...] = _dot(s, w_ref[...].astype(BF16)) + b_ref[...]


def _ada(c, w, b):
    m, d = c.shape
    n = w.shape[1]
    return pl.pallas_call(
        _ada_kernel,
        out_shape=jax.ShapeDtypeStruct((m, n), F32),
        grid=(n // ADA_TILE,),
        in_specs=[pl.BlockSpec((m, d), lambda j: (0, 0)),
                  pl.BlockSpec((d, ADA_TILE), lambda j: (0, j)),
                  pl.BlockSpec((1, ADA_TILE), lambda j: (0, j))],
        out_specs=pl.BlockSpec((m, ADA_TILE), lambda j: (0, j)),
        compiler_params=_params(("parallel",)),
        name="ada",
    )(c, w, b)


def _proj_attn_kernel(x_ref, sh_ref, sc_ref, g1_ref, w_ref, gq_ref, gk_ref, q_ref, k_ref, v_ref, kb_ref, vb_ref):
    bb, tt, d = x_ref.shape
    nh = q_ref.shape[1]
    da = nh * HEAD_DIM_A
    hb = _modulated_norm(x_ref[...], g1_ref[...], sc_ref[...], sh_ref[...]).reshape(bb * tt, d).astype(BF16)
    outs = (((q_ref,), gq_ref), ((k_ref, kb_ref), gk_ref), ((v_ref, vb_ref), None))
    for part, (out_refs, g_ref) in enumerate(outs):
        u = _dot(hb, w_ref[:, part * da:(part + 1) * da])
        for hd in range(nh):
            uh = u[:, hd * HEAD_DIM_A:(hd + 1) * HEAD_DIM_A]
            if g_ref is not None:
                uh = uh * lax.rsqrt(jnp.mean(uh * uh, axis=-1, keepdims=True) + EPS) * g_ref[...]
            for out_ref in out_refs:
                out_ref[:, hd] = uh.reshape(bb, tt, HEAD_DIM_A).astype(out_ref.dtype)


def _proj_attn(x, mod, g1, w, gq, gk):
    bn, t, d = x.shape
    nh = w.shape[1] // (3 * HEAD_DIM_A)
    bb, tt = _row_blocks(bn, t, ROW_TILE)
    head_spec = pl.BlockSpec((bb, nh, tt, HEAD_DIM_A), lambda b, i: (b, 0, i, 0))
    return pl.pallas_call(
        _proj_attn_kernel,
        out_shape=tuple(jax.ShapeDtypeStruct((bn, nh, t, HEAD_DIM_A), dt) for dt in (BF16, F32, F32, BF16, BF16)),
        grid=(bn // bb, t // tt),
        in_specs=[pl.BlockSpec((bb, tt, d), lambda b, i: (b, i, 0)),
                  pl.BlockSpec((bb, 1, d), lambda b, i: (b, 0, 0)),
                  pl.BlockSpec((bb, 1, d), lambda b, i: (b, 0, 1)),
                  pl.BlockSpec((1, d), lambda b, i: (0, 0)),
                  _resident(w.shape, lambda b, i: (0, 0)),
                  pl.BlockSpec((1, HEAD_DIM_A), lambda b, i: (0, 0)),
                  pl.BlockSpec((1, HEAD_DIM_A), lambda b, i: (0, 0))],
        out_specs=(head_spec,) * 5,
        compiler_params=_params(("parallel", "parallel")),
        name="proj_attn",
    )(x, mod, mod, g1, w, gq, gk)


def _proj_mlstm_kernel(x_ref, sh_ref, sc_ref, g1_ref, w_ref, wg_ref, wgt_ref, prev_ref, wc_ref, bc_ref,
                       qk_ref, vo_ref, gc_ref, gr_ref, new_ref, full_scr):
    i = pl.program_id(1)
    bb, tt, d = x_ref.shape
    c = qk_ref.shape[2]
    hb = _modulated_norm(x_ref[...], g1_ref[...], sc_ref[...], sh_ref[...]).reshape(bb * tt, d).astype(BF16)

    @pl.when(i == 0)
    def _():
        full_scr[:, 0:SUBLANES] = prev_ref[...]

    @pl.when(i > 0)
    def _():
        full_scr[:, 0:SUBLANES] = full_scr[:, tt:tt + SUBLANES]

    step = min(PROJ_COLS, c // 2)
    first = SUBLANES - (CONV_W - 1)
    k_scale = (c // (2 * N_HEADS_B)) ** -0.5
    for lo in range(0, c, step):
        cols = slice(lo, lo + step)
        full_scr[:, SUBLANES:SUBLANES + tt, cols] = _dot(hb, w_ref[:, cols]).reshape(bb, tt, step)
    for lo in range(0, c, step):
        cols = slice(lo, lo + step)
        vo_ref[:, :, cols] = _dot(hb, w_ref[:, c + lo:c + lo + step]).reshape(bb, tt, step).astype(vo_ref.dtype)
        y = full_scr[:, pl.ds(first, tt), cols] * wc_ref[0:1, cols]
        for j in range(1, CONV_W):
            y = y + full_scr[:, pl.ds(first + j, tt), cols] * wc_ref[j:j + 1, cols]
        y = y + bc_ref[:, cols]
        y = y * _sigmoid(y)
        qk_ref[:, :, cols] = (y * k_scale if lo >= c // 2 else y).astype(qk_ref.dtype)
    gc_ref[...] = _dot(hb, wg_ref[...]).reshape(bb, tt, LANES)
    for b in range(bb):
        gr_ref[b] = _dot_nt(wgt_ref[...], hb[b * tt:(b + 1) * tt])

    @pl.when(i == pl.num_programs(1) - 1)
    def _():
        new_ref[...] = full_scr[:, SUBLANES + tt - (CONV_W - 1):SUBLANES + tt]


def _proj_mlstm(x, mod, g1, w, wg, wgt, prev8, wc, bc):
    bn, t, d = x.shape
    c = w.shape[1] // 2
    bb, tt = _row_blocks(bn, t, ROW_TILE)
    const = lambda a: pl.BlockSpec(a.shape, lambda b, i: (0,) * a.ndim)
    return pl.pallas_call(
        _proj_mlstm_kernel,
        out_shape=(jax.ShapeDtypeStruct((bn, t, c), BF16),
                   jax.ShapeDtypeStruct((bn, t, c), BF16),
                   jax.ShapeDtypeStruct((bn, t, LANES), F32),
                   jax.ShapeDtypeStruct((bn, wgt.shape[0], t), F32),
                   jax.ShapeDtypeStruct((bn, CONV_W - 1, c), F32)),
        grid=(bn // bb, t // tt),
        in_specs=[pl.BlockSpec((bb, tt, d), lambda b, i: (b, i, 0)),
                  pl.BlockSpec((bb, 1, d), lambda b, i: (b, 0, 0)),
                  pl.BlockSpec((bb, 1, d), lambda b, i: (b, 0, 1)),
                  const(g1),
                  _resident(w.shape, lambda b, i: (0, 0)),
                  _resident(wg.shape, lambda b, i: (0, 0)),
                  _resident(wgt.shape, lambda b, i: (0, 0)),
                  pl.BlockSpec((bb, SUBLANES, c), lambda b, i: (b, 0, 0)),
                  const(wc), const(bc)],
        out_specs=(pl.BlockSpec((bb, tt, c), lambda b, i: (b, i, 0)),
                   pl.BlockSpec((bb, tt, c), lambda b, i: (b, i, 0)),
                   pl.BlockSpec((bb, tt, LANES), lambda b, i: (b, i, 0)),
                   pl.BlockSpec((bb, wgt.shape[0], tt), lambda b, i: (b, 0, i)),
                   pl.BlockSpec((bb, CONV_W - 1, c), lambda b, i: (b, 0, 0))),
        scratch_shapes=[pltpu.VMEM((bb, SUBLANES + tt, c), F32)],
        compiler_params=_params(("parallel", "arbitrary")),
        name="proj_mlstm",
    )(x, mod, mod, g1, w, wg, wgt, prev8, wc, bc)


def _strict_upper_ones(n):
    r = lax.broadcasted_iota(jnp.int32, (n, n), 0)
    c = lax.broadcasted_iota(jnp.int32, (n, n), 1)
    return (r > c).astype(BF16)


def _sb_block(q, kb, vb, tri, carry, mask):
    z = _dot_nt(q, kb) * (HEAD_DIM_A ** -0.5 * LOG2_E)
    log_beta = jnp.minimum(z, 0.0) - jnp.log2(1.0 + jnp.exp2(-jnp.abs(z)))
    log_1m = log_beta - z
    if mask is not None:
        log_1m = jnp.where(mask, log_1m, 0.0)
    tail = sum(_dot(p, tri) for p in _split_bf16(log_1m, SUFFIX_SUM_TERMS))
    total = tail[:, :1] + log_1m[:, :1]
    if carry is not None:
        tail = tail + carry
    a = jnp.exp2(log_beta + tail)
    if mask is not None:
        a = jnp.where(mask, a, 0.0)
    return _dot(a.astype(BF16), vb), total


def _attn_kernel(q_ref, kd_ref, vd_ref, kp_ref, vp_ref, o_ref, carry_ref, *, tk, past_is_prefix):
    hps, tq, dh = q_ref.shape[1], q_ref.shape[2], q_ref.shape[3]
    row = lax.broadcasted_iota(jnp.int32, (tq, tq), 0)
    col = lax.broadcasted_iota(jnp.int32, (tq, tq), 1)
    n_past = pl.program_id(2) * (tq // tk) if past_is_prefix else kp_ref.shape[2] // tk
    tri = _strict_upper_ones(tk)
    tri_diag = tri if tq == tk else _strict_upper_ones(tq)

    def past_block(h, jj, carry):
        start = pl.multiple_of(jnp.maximum(n_past - 1 - jj, 0) * tk, tk)
        kb = kp_ref[0, h, pl.ds(start, tk), :].astype(BF16)
        vb = vp_ref[0, h, pl.ds(start, tk), :].astype(BF16)
        return _sb_block(q_ref[0, h], kb, vb, tri, carry, None)

    state = []
    for h in range(hps):
        acc, s = _sb_block(q_ref[0, h], kd_ref[0, h].astype(BF16), vd_ref[0, h].astype(BF16), tri_diag, None,
                           col < row)
        pv, ds = past_block(h, 0, jnp.where(n_past > 0, s, -NO_PAST))
        state.append((acc + pv, s + ds))

    worst = None
    for h, (acc, s) in enumerate(state):
        def more(c):
            jj, _, s = c
            return jnp.logical_and(jj < n_past, jnp.max(s) > -F32_EXP2_UNDERFLOW)

        def body(c, h=h):
            jj, acc, s = c
            pv, ds = past_block(h, jj, s)
            return jj + 1, acc + pv, s + ds

        _, acc, s = lax.while_loop(more, body, (jnp.int32(1), acc, s))
        o_ref[0, :, h * dh:(h + 1) * dh] = acc.astype(o_ref.dtype)
        top = jnp.max(s, axis=0, keepdims=True)
        worst = top if worst is None else jnp.maximum(worst, top)
    carry_ref[0, 0, 0] = jnp.broadcast_to(worst, carry_ref.shape[3:])


def _attn_call(q, k, v, k_past, v_past, prefix, window):
    bn, nh, t, dh = q.shape
    if prefix:
        tq = tk = min(t, ATTN_BLOCK)
    else:
        tq, tk = t, min(k_past.shape[2], SAMPLE_KEY_BLOCK)
    p = k_past.shape[2] if window is None else window
    last = k_past.shape[2] // p - 1
    hps = min(nh, ATTN_HEADS_PER_STEP)
    blk = pl.BlockSpec((1, hps, tq, dh), lambda b, h, i: (b, h, i, 0))
    past = pl.BlockSpec((1, hps, p, dh), lambda b, h, i: (b, h, last, 0))
    return pl.pallas_call(
        functools.partial(_attn_kernel, tk=tk, past_is_prefix=prefix),
        out_shape=(jax.ShapeDtypeStruct((bn, t, nh * dh), BF16),
                   jax.ShapeDtypeStruct((bn, nh // hps, t // tq, SUBLANES, LANES), F32)),
        grid=(bn, nh // hps, t // tq),
        in_specs=[blk, blk, blk, past, past],
        out_specs=(pl.BlockSpec((1, tq, hps * dh), lambda b, h, i: (b, i, h)),
                   pl.BlockSpec((1, 1, 1, SUBLANES, LANES), lambda b, h, i: (b, h, i, 0, 0))),
        compiler_params=_params(("parallel", "parallel", "parallel")),
        name="attn",
    )(q, k, v, k_past, v_past)


def _attn(q, k, v, k_past, v_past):
    if k_past is None:
        return _attn_call(q, k, v, k, v, True, None)[0]
    near, carry = _attn_call(q, k, v, k_past, v_past, False, min(k_past.shape[2], SAMPLE_KEY_BLOCK))
    return lax.cond(jnp.max(carry) > -F32_EXP2_UNDERFLOW,
                    lambda: _attn_call(q, k, v, k_past, v_past, False, None)[0],
                    lambda: near)


def _mlstm_kernel(qk_ref, vo_ref, gc_ref, gr_ref, bc_ref, br_ref, gh_ref, c0_ref, n0_ref, m0_ref,
                  o_ref, c_out_ref, n_out_ref, m_out_ref, c_scr, n_scr, m_scr):
    ci = pl.program_id(1)
    ln = qk_ref.shape[1]
    nh = N_HEADS_B
    dh = qk_ref.shape[2] // (2 * nh)

    @pl.when(ci == 0)
    def _():
        c_scr[...] = c0_ref[0]
        n_scr[...] = n0_ref[0]
        m_scr[...] = m0_ref[0]

    gc = gc_ref[0] + bc_ref[...]
    gr = gr_ref[0] + br_ref[...]
    lfc = _log_sigmoid(gc)
    lfr = _log_sigmoid(gr)
    row = lax.broadcasted_iota(jnp.int32, (ln, ln), 0)
    col = lax.broadcasted_iota(jnp.int32, (ln, ln), 1)
    causal = col <= row
    ones_causal = causal.astype(BF16)
    ones_causal_t = (row <= col).astype(BF16)
    b_rows = sum(_dot(p, ones_causal_t) for p in _split_bf16(lfr, 3))

    for hd in range(nh):
        q = qk_ref[0, :, hd * dh:(hd + 1) * dh]
        k = qk_ref[0, :, (nh + hd) * dh:(nh + hd + 1) * dh]
        v = vo_ref[0, :, hd * dh:(hd + 1) * dh]
        og = vo_ref[0, :, (nh + hd) * dh:(nh + hd + 1) * dh].astype(F32)
        ig_col = gc[:, hd:hd + 1]
        ig_row = gr[hd:hd + 1, :]
        b_row = b_rows[nh + hd:nh + hd + 1, :]
        lf_b = jnp.broadcast_to(lfc[:, nh + hd:nh + hd + 1], (ln, ln))
        bt = sum(_dot(ones_causal, p) for p in _split_bf16(lf_b, 3))
        b_col = bt[:, :1]
        m_prev = m_scr[hd][:, :1]
        c_prev = c_scr[hd]
        n_prev = n_scr[hd]

        dmat = jnp.where(causal, (bt - b_row) + ig_row, -jnp.inf)
        inter = b_col + m_prev
        m_t = jnp.maximum(inter, jnp.max(dmat, axis=1, keepdims=True))
        w_inter = jnp.exp(inter - m_t)
        s = jnp.exp(dmat - m_t) * _dot_nt(q, k)
        num = w_inter * _dot(q, c_prev.astype(BF16)) + _dot(s.astype(BF16), v)
        den = (w_inter * jnp.sum(q.astype(F32) * n_prev, axis=1, keepdims=True)
               + jnp.sum(s, axis=1, keepdims=True))
        h = num / jnp.maximum(jnp.abs(den), jnp.exp(-m_t))
        hn = h * lax.rsqrt(jnp.mean(h * h, axis=1, keepdims=True) + EPS) * gh_ref[hd]
        o_ref[0, :, hd * dh:(hd + 1) * dh] = (hn * _sigmoid(og)).astype(o_ref.dtype)

        b_last = b_col[ln - 1:ln]
        m_new = m_t[ln - 1:ln]
        w_state = jnp.exp((b_last - b_col) + ig_col - m_new)
        decay = jnp.exp(b_last + m_prev - m_new)
        ks = k.astype(F32) * w_state
        c_scr[hd] = decay * c_prev + _dot_tn(ks.astype(BF16), v)
        n_scr[hd] = decay * n_prev + jnp.sum(ks, axis=0, keepdims=True)
        m_scr[hd] = jnp.broadcast_to(m_new, (1, LANES))

    @pl.when(ci == pl.num_programs(1) - 1)
    def _():
        c_out_ref[0] = c_scr[...]
        n_out_ref[0] = n_scr[...]
        m_out_ref[0] = m_scr[...]


def _mlstm(qk, vo, gc, gr, bias_c, bias_r, gh, c0, n0, m0):
    bn, t, c2 = qk.shape
    nh = N_HEADS_B
    dh = c2 // (2 * nh)
    ln = min(t, MLSTM_CHUNK)
    rows = gr.shape[1]
    state = lambda shape: pl.BlockSpec((1,) + shape, lambda b, i: (b, 0, 0, 0))
    return pl.pallas_call(
        _mlstm_kernel,
        out_shape=(jax.ShapeDtypeStruct((bn, t, nh * dh), BF16),
                   jax.ShapeDtypeStruct((bn, nh, dh, dh), F32),
                   jax.ShapeDtypeStruct((bn, nh, 1, dh), F32),
                   jax.ShapeDtypeStruct((bn, nh, 1, LANES), F32)),
        grid=(bn, t // ln),
        in_specs=[pl.BlockSpec((1, ln, c2), lambda b, i: (b, i, 0)),
                  pl.BlockSpec((1, ln, c2), lambda b, i: (b, i, 0)),
                  pl.BlockSpec((1, ln, LANES), lambda b, i: (b, i, 0)),
                  pl.BlockSpec((1, rows, ln), lambda b, i: (b, 0, i)),
                  pl.BlockSpec((1, LANES), lambda b, i: (0, 0)),
                  pl.BlockSpec((rows, 1), lambda b, i: (0, 0)),
                  pl.BlockSpec((nh, 1, dh), lambda b, i: (0, 0, 0)),
                  state((nh, dh, dh)), state((nh, 1, dh)), state((nh, 1, LANES))],
        out_specs=(pl.BlockSpec((1, ln, nh * dh), lambda b, i: (b, i, 0)),
                   state((nh, dh, dh)), state((nh, 1, dh)), state((nh, 1, LANES))),
        scratch_shapes=[pltpu.VMEM((nh, dh, dh), F32), pltpu.VMEM((nh, 1, dh), F32),
                        pltpu.VMEM((nh, 1, LANES), F32)],
        compiler_params=_params(("parallel", "arbitrary")),
        name="mlstm",
    )(qk, vo, gc, gr, bias_c, bias_r, gh, c0, n0, m0)


def _out_proj_kernel(x_ref, oa_ref, ob_ref, gt_ref, w_ref, o_ref):
    bb, tt, d = x_ref.shape
    da, db = oa_ref.shape[2], ob_ref.shape[2]
    y = (_dot(oa_ref[...].reshape(bb * tt, da), w_ref[0:da])
         + _dot(ob_ref[...].reshape(bb * tt, db), w_ref[da:da + db]))
    o_ref[...] = x_ref[...] + gt_ref[...] * y.reshape(bb, tt, d)


def _out_proj(x, oa, ob, mod, w):
    bn, t, d = x.shape
    bb, tt = _row_blocks(bn, t, ROW_TILE)
    rows = lambda width: pl.BlockSpec((bb, tt, width), lambda b, i: (b, i, 0))
    return pl.pallas_call(
        _out_proj_kernel,
        out_shape=jax.ShapeDtypeStruct((bn, t, d), F32),
        grid=(bn // bb, t // tt),
        in_specs=[rows(d), rows(oa.shape[2]), rows(ob.shape[2]),
                  pl.BlockSpec((bb, 1, d), lambda b, i: (b, 0, 2)),
                  _resident(w.shape, lambda b, i: (0, 0))],
        out_specs=rows(d),
        compiler_params=_params(("parallel", "parallel")),
        name="out_proj",
    )(x, oa, ob, mod, w)


def _ffn_kernel(x_ref, sh_ref, sc_ref, gt_ref, g2_ref, w1_ref, w2_ref, o_ref, h_scr, acc_scr):
    f = pl.program_id(2)
    bb, tt, d = x_ref.shape

    @pl.when(f == 0)
    def _():
        h = _modulated_norm(x_ref[...], g2_ref[...], sc_ref[...], sh_ref[...])
        h_scr[...] = h.reshape(bb * tt, d).astype(BF16)
        acc_scr[...] = jnp.zeros_like(acc_scr)

    a = jnp.maximum(_dot(h_scr[...], w1_ref[...]), 0.0)
    acc_scr[...] += _dot((a * a).astype(BF16), w2_ref[...])

    @pl.when(f == pl.num_programs(2) - 1)
    def _():
        o_ref[...] = x_ref[...] + gt_ref[...] * acc_scr[...].reshape(bb, tt, d)


def _ffn(x, mod, g2, w1, w2):
    bn, t, d = x.shape
    dff = w1.shape[1]
    bb, tt = _row_blocks(bn, t, ROW_TILE)
    rows = pl.BlockSpec((bb, tt, d), lambda b, i, f: (b, i, 0))
    chunk = lambda c: pl.BlockSpec((bb, 1, d), lambda b, i, f: (b, 0, c))
    return pl.pallas_call(
        _ffn_kernel,
        out_shape=jax.ShapeDtypeStruct((bn, t, d), F32),
        grid=(bn // bb, t // tt, dff // FF_TILE),
        in_specs=[rows, chunk(3), chunk(4), chunk(5),
                  pl.BlockSpec((1, d), lambda b, i, f: (0, 0)),
                  pl.BlockSpec((d, FF_TILE), lambda b, i, f: (0, f)),
                  pl.BlockSpec((FF_TILE, d), lambda b, i, f: (f, 0))],
        out_specs=rows,
        scratch_shapes=[pltpu.VMEM((bb * tt, d), BF16), pltpu.VMEM((bb * tt, d), F32)],
        compiler_params=_params(("parallel", "parallel", "arbitrary")),
        name="ffn",
    )(x, mod, mod, mod, g2, w1, w2)


def _layer(x, mod, past, wts):
    bn, t, d = x.shape
    da = wts["w_in_a"].shape[1] // 3
    db = wts["w_in_b"].shape[1] // 4
    nh, dh = N_HEADS_B, db // N_HEADS_B
    if past is None:
        k_past = v_past = None
        conv_prev = jnp.zeros((bn, CONV_W - 1, 2 * db), F32)
        c0 = jnp.zeros((bn, nh, dh, dh), F32)
        n0 = jnp.zeros((bn, nh, dh), F32)
        m0 = jnp.zeros((bn, nh), F32)
    else:
        k_past, v_past, c0, n0, m0, conv_prev = (a.astype(F32) for a in past)

    q, k, v, k_bf, v_bf = _proj_attn(x, mod, wts["g_norm1"], wts["w_in_a"], wts["g_q"], wts["g_k"])
    prev8 = jnp.pad(conv_prev, ((0, 0), (SUBLANES - (CONV_W - 1), 0), (0, 0)))
    qk, vo, gc, gr, conv_new = _proj_mlstm(x, mod, wts["g_norm1"], wts["w_in_b"], wts["w_gate"], wts["w_gate_t"],
                                           prev8, wts["w_conv"], wts["b_conv"])
    o_a = _attn(q, k_bf, v_bf, k_past, v_past)
    o_b, c_new, n_new, m_new = _mlstm(
        qk, vo, gc, gr, wts["gate_bias_c"], wts["gate_bias_r"], wts["g_h"], c0, n0.reshape(bn, nh, 1, dh),
        jnp.broadcast_to(m0[:, :, None, None], (bn, nh, 1, LANES)))

    x = _out_proj(x, o_a, o_b, mod, wts["w_out"])
    x = _ffn(x, mod, wts["g_norm2"], wts["w_ff1"], wts["w_ff2"])
    return x, (k, v, c_new, n_new.reshape(bn, nh, dh), m_new[:, :, 0, 0], conv_new)


def _layer_weights(l, w_in, g_norm1, g_q, g_k, w_conv, b_conv, b_i, b_f, g_h, w_out, g_norm2, w_ff1, w_ff2):
    d = w_in.shape[1]
    nh = N_HEADS_B
    da = 3 * (d // 2)
    n_main = w_in.shape[2] - 2 * nh
    w_in_bf = w_in[l].astype(BF16)
    w_gate = w_in[l][:, n_main:]
    bias = jnp.concatenate([b_i[l], b_f[l]])
    rows = 2 * SUBLANES
    return {
        "w_in_a": w_in_bf[:, :da],
        "w_in_b": w_in_bf[:, da:n_main],
        "w_gate": jnp.pad(w_gate, ((0, 0), (0, LANES - 2 * nh))).astype(BF16),
        "w_gate_t": jnp.pad(w_gate.T, ((0, rows - 2 * nh), (0, 0))).astype(BF16),
        "gate_bias_c": jnp.pad(bias, (0, LANES - 2 * nh)).reshape(1, LANES),
        "gate_bias_r": jnp.pad(bias, (0, rows - 2 * nh)).reshape(rows, 1),
        "g_norm1": g_norm1[l].reshape(1, d), "g_norm2": g_norm2[l].reshape(1, d),
        "g_q": g_q[l].reshape(1, -1), "g_k": g_k[l].reshape(1, -1),
        "w_conv": w_conv[l], "b_conv": b_conv[l].reshape(1, -1),
        "g_h": g_h[l].reshape(nh, 1, -1),
        "w_out": w_out[l].astype(BF16), "w_ff1": w_ff1[l].astype(BF16), "w_ff2": w_ff2[l].astype(BF16),
    }


def kernel(x_prompt, x_sample, c_prompt, c_sample, cache_k, cache_v, state_C, state_n, state_m, state_conv,
           w_ada, b_ada, g_norm1, w_in, g_q, g_k, w_conv, b_conv, b_i, b_f, g_h, w_out, g_norm2, w_ff1, w_ff2):
    depth = w_ada.shape[0]
    bp, bs = c_prompt.shape[0], c_sample.shape[0]
    c_rows = -(-(bp + bs) // (2 * SUBLANES)) * (2 * SUBLANES)
    c_all = jnp.pad(jnp.concatenate([c_prompt, c_sample], axis=0), ((0, c_rows - bp - bs), (0, 0)))
    y_prompt, y_sample = x_prompt, x_sample
    new_p, new_s = [], []
    for l in range(depth):
        wts = _layer_weights(l, w_in, g_norm1, g_q, g_k, w_conv, b_conv, b_i, b_f, g_h, w_out, g_norm2,
                             w_ff1, w_ff2)
        mod = _ada(c_all, w_ada[l], b_ada[l].reshape(1, -1))
        mod_p = mod[:bp].reshape(bp, 1, -1)
        mod_s = mod[bp:bp + bs].reshape(bs, 1, -1)
        y_prompt, sp = _layer(y_prompt, mod_p, None, wts)
        past = (cache_k[l], cache_v[l], state_C[l], state_n[l], state_m[l], state_conv[l])
        y_sample, ss = _layer(y_sample, mod_s, past, wts)
        new_p.append(sp)
        new_s.append(ss)
    stack = lambda states, i: jnp.stack([s[i] for s in states], axis=0)
    return ((y_prompt, y_sample) + tuple(stack(new_p, i) for i in range(6))
            + tuple(stack(new_s, i) for i in range(6)))
```

```python
import functools

import jax
import jax.numpy as jnp
from jax import lax
from jax.experimental import pallas as pl
from jax.experimental.pallas import tpu as pltpu

F32, BF16 = jnp.float32, jnp.bfloat16
EPS = 1e-6
HEAD_DIM_A = 128
N_HEADS_B = 4
CONV_W = 4
LANES = 128
SUBLANES = 8
F32_EXP2_UNDERFLOW = 151.0
LOG2_E = 1.4426950408889634
NO_PAST = 1e30
V7X_VMEM_BYTES = 64 * 2 ** 20
VMEM_LIMIT = 56 * 2 ** 20

ROW_TILE = 512
ATTN_BLOCK = 256
SAMPLE_KEY_BLOCK = 256
MLSTM_CHUNK = 256
FF_TILE = 1024
ADA_TILE = 1024
ATTN_HEADS_PER_STEP = 4
SUFFIX_SUM_TERMS = 1
PROJ_COLS = 256


def _params(semantics, vmem_limit=VMEM_LIMIT):
    return pltpu.CompilerParams(dimension_semantics=semantics, vmem_limit_bytes=vmem_limit)


def _resident(shape, index_map):
    return pl.BlockSpec(shape, index_map, pipeline_mode=pl.Buffered(1))


def _sigmoid(x):
    return 1.0 / (1.0 + jnp.exp(-x))


def _log_sigmoid(x):
    return jnp.minimum(x, 0.0) - jnp.log1p(jnp.exp(-jnp.abs(x)))


def _dot(a, b):
    return jnp.dot(a, b, preferred_element_type=F32)


def _dot_nt(a, b):
    return lax.dot_general(a, b, (((1,), (1,)), ((), ())), preferred_element_type=F32)


def _dot_tn(a, b):
    return lax.dot_general(a, b, (((0,), (0,)), ((), ())), preferred_element_type=F32)


def _split_bf16(x, parts):
    out = []
    for _ in range(parts - 1):
        p = x.astype(BF16)
        out.append(p)
        x = x - p.astype(F32)
    out.append(x.astype(BF16))
    return out


def _modulated_norm(x, g, sc, sh):
    y = x * lax.rsqrt(jnp.mean(x * x, axis=-1, keepdims=True) + EPS)
    return (y * g) * (1.0 + sc) + sh


def _row_blocks(bn, t, rows):
    if t >= rows:
        return 1, rows
    return min(bn, rows // t), t


def _ada_kernel(c_ref, w_ref, b_ref, o_ref):
    c = c_ref[...]
    s = (c * _sigmoid(c)).astype(BF16)
    o_ref[...] = _dot(s, w_ref[...].astype(BF16)) + b_ref[...]


def _ada(c, w, b):
    m, d = c.shape
    n = w.shape[1]
    return pl.pallas_call(
        _ada_kernel,
        out_shape=jax.ShapeDtypeStruct((m, n), F32),
        grid=(n // ADA_TILE,),
        in_specs=[pl.BlockSpec((m, d), lambda j: (0, 0)),
                  pl.BlockSpec((d, ADA_TILE), lambda j: (0, j)),
                  pl.BlockSpec((1, ADA_TILE), lambda j: (0, j))],
        out_specs=pl.BlockSpec((m, ADA_TILE), lambda j: (0, j)),
        compiler_params=_params(("parallel",)),
        name="ada",
    )(c, w, b)


def _proj_attn_kernel(x_ref, sh_ref, sc_ref, g1_ref, w_ref, gq_ref, gk_ref, q_ref, k_ref, v_ref, kb_ref, vb_ref):
    bb, tt, d = x_ref.shape
    nh = q_ref.shape[1]
    da = nh * HEAD_DIM_A
    hb = _modulated_norm(x_ref[...], g1_ref[...], sc_ref[...], sh_ref[...]).reshape(bb * tt, d).astype(BF16)
    outs = (((q_ref,), gq_ref), ((k_ref, kb_ref), gk_ref), ((v_ref, vb_ref), None))
    for part, (out_refs, g_ref) in enumerate(outs):
        u = _dot(hb, w_ref[:, part * da:(part + 1) * da])
        for hd in range(nh):
            uh = u[:, hd * HEAD_DIM_A:(hd + 1) * HEAD_DIM_A]
            if g_ref is not None:
                uh = uh * lax.rsqrt(jnp.mean(uh * uh, axis=-1, keepdims=True) + EPS) * g_ref[...]
            for out_ref in out_refs:
                out_ref[:, hd] = uh.reshape(bb, tt, HEAD_DIM_A).astype(out_ref.dtype)


def _proj_attn(x, mod, g1, w, gq, gk):
    bn, t, d = x.shape
    nh = d // 2 // HEAD_DIM_A
    bb, tt = _row_blocks(bn, t, ROW_TILE)
    head_spec = pl.BlockSpec((bb, nh, tt, HEAD_DIM_A), lambda b, i: (b, 0, i, 0))
    return pl.pallas_call(
        _proj_attn_kernel,
        out_shape=tuple(jax.ShapeDtypeStruct((bn, nh, t, HEAD_DIM_A), dt) for dt in (BF16, F32, F32, BF16, BF16)),
        grid=(bn // bb, t // tt),
        in_specs=[pl.BlockSpec((bb, tt, d), lambda b, i: (b, i, 0)),
                  pl.BlockSpec((bb, 1, d), lambda b, i: (b, 0, 0)),
                  pl.BlockSpec((bb, 1, d), lambda b, i: (b, 0, 1)),
                  pl.BlockSpec((1, d), lambda b, i: (0, 0)),
                  _resident((d, 3 * nh * HEAD_DIM_A), lambda b, i: (0, 0)),
                  pl.BlockSpec((1, HEAD_DIM_A), lambda b, i: (0, 0)),
                  pl.BlockSpec((1, HEAD_DIM_A), lambda b, i: (0, 0))],
        out_specs=(head_spec,) * 5,
        compiler_params=_params(("parallel", "parallel")),
        name="proj_attn",
    )(x, mod, mod, g1, w, gq, gk)


def _proj_mlstm_kernel(x_ref, sh_ref, sc_ref, g1_ref, wq_ref, wk_ref, wv_ref, wo_ref, wg_ref, wgt_ref, prev_ref,
                       wc_ref, bc_ref, qk_ref, vo_ref, gc_ref, gr_ref, new_ref, halo_scr):
    i = pl.program_id(1)
    bb, tt, d = x_ref.shape
    c = qk_ref.shape[2]
    hb = _modulated_norm(x_ref[...], g1_ref[...], sc_ref[...], sh_ref[...]).reshape(bb * tt, d).astype(BF16)

    @pl.when(i == 0)
    def _():
        halo_scr[...] = prev_ref[...]

    db = c // 2
    step = min(PROJ_COLS, db)
    k_scale = (db // N_HEADS_B) ** -0.5

    def w_cols(lo):
        return (wq_ref, wk_ref, wv_ref, wo_ref)[lo // db][:, lo % db:lo % db + step]

    for lo in range(0, c, step):
        cols = slice(lo, lo + step)
        u = _dot(hb, w_cols(lo)).reshape(bb, tt, step)
        ext = jnp.concatenate([halo_scr[:, :, cols], u], axis=1)
        y = pltpu.roll(ext, CONV_W - 1, axis=1)[:, SUBLANES:] * wc_ref[0:1, cols]
        for j in range(1, CONV_W - 1):
            y = y + pltpu.roll(ext, CONV_W - 1 - j, axis=1)[:, SUBLANES:] * wc_ref[j:j + 1, cols]
        y = y + u * wc_ref[CONV_W - 1:CONV_W, cols] + bc_ref[:, cols]
        y = y * _sigmoid(y)
        qk_ref[:, :, cols] = (y * k_scale if lo >= c // 2 else y).astype(qk_ref.dtype)
        halo_scr[:, :, cols] = ext[:, tt:tt + SUBLANES]
    for lo in range(0, c, step):
        vo_ref[:, :, lo:lo + step] = _dot(hb, w_cols(c + lo)).reshape(bb, tt, step).astype(vo_ref.dtype)
    gc_ref[...] = _dot(hb, wg_ref[...]).reshape(bb, tt, LANES)
    for b in range(bb):
        gr_ref[b] = _dot_nt(wgt_ref[...], hb[b * tt:(b + 1) * tt])

    @pl.when(i == pl.num_programs(1) - 1)
    def _():
        new_ref[...] = halo_scr[:, SUBLANES - (CONV_W - 1):SUBLANES]


def _proj_mlstm(x, mod, g1, w, wg, wgt, prev8, wc, bc):
    bn, t, d = x.shape
    db = d - d // 2
    c = 2 * db
    first = (3 * (d // 2)) // db
    bb, tt = _row_blocks(bn, t, ROW_TILE)
    const = lambda a: pl.BlockSpec(a.shape, lambda b, i: (0,) * a.ndim)
    return pl.pallas_call(
        _proj_mlstm_kernel,
        out_shape=(jax.ShapeDtypeStruct((bn, t, c), BF16),
                   jax.ShapeDtypeStruct((bn, t, c), BF16),
                   jax.ShapeDtypeStruct((bn, t, LANES), F32),
                   jax.ShapeDtypeStruct((bn, wgt.shape[0], t), F32),
                   jax.ShapeDtypeStruct((bn, CONV_W - 1, c), F32)),
        grid=(bn // bb, t // tt),
        in_specs=[pl.BlockSpec((bb, tt, d), lambda b, i: (b, i, 0)),
                  pl.BlockSpec((bb, 1, d), lambda b, i: (b, 0, 0)),
                  pl.BlockSpec((bb, 1, d), lambda b, i: (b, 0, 1)),
                  const(g1),
                  *[_resident((d, db), lambda b, i, j=j: (0, first + j)) for j in range(4)],
                  _resident(wg.shape, lambda b, i: (0, 0)),
                  _resident(wgt.shape, lambda b, i: (0, 0)),
                  pl.BlockSpec((bb, SUBLANES, c), lambda b, i: (b, 0, 0)),
                  const(wc), const(bc)],
        out_specs=(pl.BlockSpec((bb, tt, c), lambda b, i: (b, i, 0)),
                   pl.BlockSpec((bb, tt, c), lambda b, i: (b, i, 0)),
                   pl.BlockSpec((bb, tt, LANES), lambda b, i: (b, i, 0)),
                   pl.BlockSpec((bb, wgt.shape[0], tt), lambda b, i: (b, 0, i)),
                   pl.BlockSpec((bb, CONV_W - 1, c), lambda b, i: (b, 0, 0))),
        scratch_shapes=[pltpu.VMEM((bb, SUBLANES, c), F32)],
        compiler_params=_params(("parallel", "arbitrary")),
        name="proj_mlstm",
    )(x, mod, mod, g1, w, w, w, w, wg, wgt, prev8, wc, bc)


def _strict_upper_ones(n):
    r = lax.broadcasted_iota(jnp.int32, (n, n), 0)
    c = lax.broadcasted_iota(jnp.int32, (n, n), 1)
    return (r > c).astype(BF16)


def _sb_block(q, kb, vb, tri, carry, mask):
    z = _dot_nt(q, kb) * (HEAD_DIM_A ** -0.5 * LOG2_E)
    log_beta = jnp.minimum(z, 0.0) - jnp.log2(1.0 + jnp.exp2(-jnp.abs(z)))
    log_1m = log_beta - z
    if mask is not None:
        log_1m = jnp.where(mask, log_1m, 0.0)
    tail = sum(_dot(p, tri) for p in _split_bf16(log_1m, SUFFIX_SUM_TERMS))
    total = tail[:, :1] + log_1m[:, :1]
    if carry is not None:
        tail = tail + carry
    a = jnp.exp2(log_beta + tail)
    if mask is not None:
        a = jnp.where(mask, a, 0.0)
    return _dot(a.astype(BF16), vb), total


def _attn_kernel(q_ref, kd_ref, vd_ref, kp_ref, vp_ref, o_ref, carry_ref, *, tk, past_is_prefix):
    hps, tq, dh = q_ref.shape[1], q_ref.shape[2], q_ref.shape[3]
    row = lax.broadcasted_iota(jnp.int32, (tq, tq), 0)
    col = lax.broadcasted_iota(jnp.int32, (tq, tq), 1)
    n_past = pl.program_id(2) * (tq // tk) if past_is_prefix else kp_ref.shape[2] // tk
    tri = _strict_upper_ones(tk)
    tri_diag = tri if tq == tk else _strict_upper_ones(tq)

    def past_block(h, jj, carry):
        start = pl.multiple_of(jnp.maximum(n_past - 1 - jj, 0) * tk, tk)
        kb = kp_ref[0, h, pl.ds(start, tk), :].astype(BF16)
        vb = vp_ref[0, h, pl.ds(start, tk), :].astype(BF16)
        return _sb_block(q_ref[0, h], kb, vb, tri, carry, None)

    state = []
    for h in range(hps):
        acc, s = _sb_block(q_ref[0, h], kd_ref[0, h].astype(BF16), vd_ref[0, h].astype(BF16), tri_diag, None,
                           col < row)
        pv, ds = past_block(h, 0, jnp.where(n_past > 0, s, -NO_PAST))
        state.append((acc + pv, s + ds))

    worst = None
    for h, (acc, s) in enumerate(state):
        def more(c):
            jj, _, s = c
            return jnp.logical_and(jj < n_past, jnp.max(s) > -F32_EXP2_UNDERFLOW)

        def body(c, h=h):
            jj, acc, s = c
            pv, ds = past_block(h, jj, s)
            return jj + 1, acc + pv, s + ds

        _, acc, s = lax.while_loop(more, body, (jnp.int32(1), acc, s))
        o_ref[0, :, h * dh:(h + 1) * dh] = acc.astype(o_ref.dtype)
        top = jnp.max(s, axis=0, keepdims=True)
        worst = top if worst is None else jnp.maximum(worst, top)
    carry_ref[0, 0, 0] = jnp.broadcast_to(worst, carry_ref.shape[3:])


def _attn_call(q, k, v, k_past, v_past, prefix, window):
    bn, nh, t, dh = q.shape
    if prefix:
        tq = tk = min(t, ATTN_BLOCK)
    else:
        tq, tk = t, min(k_past.shape[2], SAMPLE_KEY_BLOCK)
    p = k_past.shape[2] if window is None else window
    last = k_past.shape[2] // p - 1
    hps = min(nh, ATTN_HEADS_PER_STEP)
    blk = pl.BlockSpec((1, hps, tq, dh), lambda b, h, i: (b, h, i, 0))
    past = pl.BlockSpec((1, hps, p, dh), lambda b, h, i: (b, h, last, 0))
    return pl.pallas_call(
        functools.partial(_attn_kernel, tk=tk, past_is_prefix=prefix),
        out_shape=(jax.ShapeDtypeStruct((bn, t, nh * dh), BF16),
                   jax.ShapeDtypeStruct((bn, nh // hps, t // tq, SUBLANES, LANES), F32)),
        grid=(bn, nh // hps, t // tq),
        in_specs=[blk, blk, blk, past, past],
        out_specs=(pl.BlockSpec((1, tq, hps * dh), lambda b, h, i: (b, i, h)),
                   pl.BlockSpec((1, 1, 1, SUBLANES, LANES), lambda b, h, i: (b, h, i, 0, 0))),
        compiler_params=_params(("parallel", "parallel", "parallel")),
        name="attn",
    )(q, k, v, k_past, v_past)


def _attn(q, k, v, k_past, v_past):
    if k_past is None:
        return _attn_call(q, k, v, k, v, True, None)[0]
    near, carry = _attn_call(q, k, v, k_past, v_past, False, min(k_past.shape[2], SAMPLE_KEY_BLOCK))
    return lax.cond(jnp.max(carry) > -F32_EXP2_UNDERFLOW,
                    lambda: _attn_call(q, k, v, k_past, v_past, False, None)[0],
                    lambda: near)


def _mlstm_kernel(qk_ref, vo_ref, gc_ref, gr_ref, bc_ref, br_ref, gh_ref, c0_ref, n0_ref, m0_ref,
                  o_ref, c_out_ref, n_out_ref, m_out_ref, c_scr, n_scr, m_scr):
    ci = pl.program_id(1)
    ln = qk_ref.shape[1]
    nh = N_HEADS_B
    dh = qk_ref.shape[2] // (2 * nh)

    @pl.when(ci == 0)
    def _():
        c_scr[...] = c0_ref[0]
        n_scr[...] = n0_ref[0]
        m_scr[...] = m0_ref[0]

    gc = gc_ref[0] + bc_ref[...]
    gr = gr_ref[0] + br_ref[...]
    lfc = _log_sigmoid(gc)
    lfr = _log_sigmoid(gr)
    row = lax.broadcasted_iota(jnp.int32, (ln, ln), 0)
    col = lax.broadcasted_iota(jnp.int32, (ln, ln), 1)
    causal = col <= row
    ones_causal = causal.astype(BF16)
    ones_causal_t = (row <= col).astype(BF16)
    b_rows = sum(_dot(p, ones_causal_t) for p in _split_bf16(lfr, 3))

    for hd in range(nh):
        q = qk_ref[0, :, hd * dh:(hd + 1) * dh]
        k = qk_ref[0, :, (nh + hd) * dh:(nh + hd + 1) * dh]
        v = vo_ref[0, :, hd * dh:(hd + 1) * dh]
        og = vo_ref[0, :, (nh + hd) * dh:(nh + hd + 1) * dh].astype(F32)
        ig_col = gc[:, hd:hd + 1]
        ig_row = gr[hd:hd + 1, :]
        b_row = b_rows[nh + hd:nh + hd + 1, :]
        lf_b = jnp.broadcast_to(lfc[:, nh + hd:nh + hd + 1], (ln, ln))
        bt = sum(_dot(ones_causal, p) for p in _split_bf16(lf_b, 3))
        b_col = bt[:, :1]
        m_prev = m_scr[hd][:, :1]
        c_prev = c_scr[hd]
        n_prev = n_scr[hd]

        dmat = jnp.where(causal, (bt - b_row) + ig_row, -jnp.inf)
        inter = b_col + m_prev
        m_t = jnp.maximum(inter, jnp.max(dmat, axis=1, keepdims=True))
        w_inter = jnp.exp(inter - m_t)
        s = jnp.exp(dmat - m_t) * _dot_nt(q, k)
        num = w_inter * _dot(q, c_prev.astype(BF16)) + _dot(s.astype(BF16), v)
        den = (w_inter * jnp.sum(q.astype(F32) * n_prev, axis=1, keepdims=True)
               + jnp.sum(s, axis=1, keepdims=True))
        h = num / jnp.maximum(jnp.abs(den), jnp.exp(-m_t))
        hn = h * lax.rsqrt(jnp.mean(h * h, axis=1, keepdims=True) + EPS) * gh_ref[hd]
        o_ref[0, :, hd * dh:(hd + 1) * dh] = (hn * _sigmoid(og)).astype(o_ref.dtype)

        b_last = b_col[ln - 1:ln]
        m_new = m_t[ln - 1:ln]
        w_state = jnp.exp((b_last - b_col) + ig_col - m_new)
        decay = jnp.exp(b_last + m_prev - m_new)
        ks = k.astype(F32) * w_state
        c_scr[hd] = decay * c_prev + _dot_tn(ks.astype(BF16), v)
        n_scr[hd] = decay * n_prev + jnp.sum(ks, axis=0, keepdims=True)
        m_scr[hd] = jnp.broadcast_to(m_new, (1, LANES))

    @pl.when(ci == pl.num_programs(1) - 1)
    def _():
        c_out_ref[0] = c_scr[...]
        n_out_ref[0] = n_scr[...]
        m_out_ref[0] = m_scr[...]


def _mlstm(qk, vo, gc, gr, bias_c, bias_r, gh, c0, n0, m0):
    bn, t, c2 = qk.shape
    nh = N_HEADS_B
    dh = c2 // (2 * nh)
    ln = min(t, MLSTM_CHUNK)
    rows = gr.shape[1]
    state = lambda shape: pl.BlockSpec((1,) + shape, lambda b, i: (b, 0, 0, 0))
    return pl.pallas_call(
        _mlstm_kernel,
        out_shape=(jax.ShapeDtypeStruct((bn, t, nh * dh), BF16),
                   jax.ShapeDtypeStruct((bn, nh, dh, dh), F32),
                   jax.ShapeDtypeStruct((bn, nh, 1, dh), F32),
                   jax.ShapeDtypeStruct((bn, nh, 1, LANES), F32)),
        grid=(bn, t // ln),
        in_specs=[pl.BlockSpec((1, ln, c2), lambda b, i: (b, i, 0)),
                  pl.BlockSpec((1, ln, c2), lambda b, i: (b, i, 0)),
                  pl.BlockSpec((1, ln, LANES), lambda b, i: (b, i, 0)),
                  pl.BlockSpec((1, rows, ln), lambda b, i: (b, 0, i)),
                  pl.BlockSpec((1, LANES), lambda b, i: (0, 0)),
                  pl.BlockSpec((rows, 1), lambda b, i: (0, 0)),
                  pl.BlockSpec((nh, 1, dh), lambda b, i: (0, 0, 0)),
                  state((nh, dh, dh)), state((nh, 1, dh)), state((nh, 1, LANES))],
        out_specs=(pl.BlockSpec((1, ln, nh * dh), lambda b, i: (b, i, 0)),
                   state((nh, dh, dh)), state((nh, 1, dh)), state((nh, 1, LANES))),
        scratch_shapes=[pltpu.VMEM((nh, dh, dh), F32), pltpu.VMEM((nh, 1, dh), F32),
                        pltpu.VMEM((nh, 1, LANES), F32)],
        compiler_params=_params(("parallel", "arbitrary")),
        name="mlstm",
    )(qk, vo, gc, gr, bias_c, bias_r, gh, c0, n0, m0)


def _out_proj_kernel(x_ref, oa_ref, ob_ref, gt_ref, w_ref, o_ref):
    bb, tt, d = x_ref.shape
    da, db = oa_ref.shape[2], ob_ref.shape[2]
    y = (_dot(oa_ref[...].reshape(bb * tt, da), w_ref[0:da])
         + _dot(ob_ref[...].reshape(bb * tt, db), w_ref[da:da + db]))
    o_ref[...] = x_ref[...] + gt_ref[...] * y.reshape(bb, tt, d)


def _out_proj(x, oa, ob, mod, w):
    bn, t, d = x.shape
    bb, tt = _row_blocks(bn, t, ROW_TILE)
    rows = lambda width: pl.BlockSpec((bb, tt, width), lambda b, i: (b, i, 0))
    return pl.pallas_call(
        _out_proj_kernel,
        out_shape=jax.ShapeDtypeStruct((bn, t, d), F32),
        grid=(bn // bb, t // tt),
        in_specs=[rows(d), rows(oa.shape[2]), rows(ob.shape[2]),
                  pl.BlockSpec((bb, 1, d), lambda b, i: (b, 0, 2)),
                  _resident(w.shape, lambda b, i: (0, 0))],
        out_specs=rows(d),
        compiler_params=_params(("parallel", "parallel")),
        name="out_proj",
    )(x, oa, ob, mod, w)


def _ffn_kernel(x_ref, sh_ref, sc_ref, gt_ref, g2_ref, w1_ref, w2_ref, o_ref, h_scr, acc_scr):
    f = pl.program_id(2)
    bb, tt, d = x_ref.shape

    @pl.when(f == 0)
    def _():
        h = _modulated_norm(x_ref[...], g2_ref[...], sc_ref[...], sh_ref[...])
        h_scr[...] = h.reshape(bb * tt, d).astype(BF16)
        acc_scr[...] = jnp.zeros_like(acc_scr)

    a = jnp.maximum(_dot(h_scr[...], w1_ref[...]), 0.0)
    acc_scr[...] += _dot((a * a).astype(BF16), w2_ref[...])

    @pl.when(f == pl.num_programs(2) - 1)
    def _():
        o_ref[...] = x_ref[...] + gt_ref[...] * acc_scr[...].reshape(bb, tt, d)


def _ffn(x, mod, g2, w1, w2):
    bn, t, d = x.shape
    dff = w1.shape[1]
    bb, tt = _row_blocks(bn, t, ROW_TILE)
    rows = pl.BlockSpec((bb, tt, d), lambda b, i, f: (b, i, 0))
    chunk = lambda c: pl.BlockSpec((bb, 1, d), lambda b, i, f: (b, 0, c))
    return pl.pallas_call(
        _ffn_kernel,
        out_shape=jax.ShapeDtypeStruct((bn, t, d), F32),
        grid=(bn // bb, t // tt, dff // FF_TILE),
        in_specs=[rows, chunk(3), chunk(4), chunk(5),
                  pl.BlockSpec((1, d), lambda b, i, f: (0, 0)),
                  pl.BlockSpec((d, FF_TILE), lambda b, i, f: (0, f)),
                  pl.BlockSpec((FF_TILE, d), lambda b, i, f: (f, 0))],
        out_specs=rows,
        scratch_shapes=[pltpu.VMEM((bb * tt, d), BF16), pltpu.VMEM((bb * tt, d), F32)],
        compiler_params=_params(("parallel", "parallel", "arbitrary")),
        name="ffn",
    )(x, mod, mod, mod, g2, w1, w2)


def _layer(x, mod, past, wts):
    bn, t, d = x.shape
    db = d - d // 2
    nh, dh = N_HEADS_B, db // N_HEADS_B
    if past is None:
        k_past = v_past = None
        conv_prev = jnp.zeros((bn, CONV_W - 1, 2 * db), F32)
        c0 = jnp.zeros((bn, nh, dh, dh), F32)
        n0 = jnp.zeros((bn, nh, dh), F32)
        m0 = jnp.zeros((bn, nh), F32)
    else:
        k_past, v_past, c0, n0, m0, conv_prev = (a.astype(F32) for a in past)

    q, k, v, k_bf, v_bf = _proj_attn(x, mod, wts["g_norm1"], wts["w_in"], wts["g_q"], wts["g_k"])
    prev8 = jnp.pad(conv_prev, ((0, 0), (SUBLANES - (CONV_W - 1), 0), (0, 0)))
    qk, vo, gc, gr, conv_new = _proj_mlstm(x, mod, wts["g_norm1"], wts["w_in"], wts["w_gate"], wts["w_gate_t"],
                                           prev8, wts["w_conv"], wts["b_conv"])
    o_a = _attn(q, k_bf, v_bf, k_past, v_past)
    o_b, c_new, n_new, m_new = _mlstm(
        qk, vo, gc, gr, wts["gate_bias_c"], wts["gate_bias_r"], wts["g_h"], c0, n0.reshape(bn, nh, 1, dh),
        jnp.broadcast_to(m0[:, :, None, None], (bn, nh, 1, LANES)))

    x = _out_proj(x, o_a, o_b, mod, wts["w_out"])
    x = _ffn(x, mod, wts["g_norm2"], wts["w_ff1"], wts["w_ff2"])
    return x, (k, v, c_new, n_new.reshape(bn, nh, dh), m_new[:, :, 0, 0], conv_new)


def _layer_weights(l, w_in, g_norm1, g_q, g_k, w_conv, b_conv, b_i, b_f, g_h, w_out, g_norm2, w_ff1, w_ff2):
    d = w_in.shape[1]
    nh = N_HEADS_B
    w_gate = w_in[l][:, w_in.shape[2] - 2 * nh:]
    bias = jnp.concatenate([b_i[l], b_f[l]])
    rows = 2 * SUBLANES
    return {
        "w_in": w_in[l].astype(BF16),
        "w_gate": jnp.pad(w_gate, ((0, 0), (0, LANES - 2 * nh))).astype(BF16),
        "w_gate_t": jnp.pad(w_gate.T, ((0, rows - 2 * nh), (0, 0))).astype(BF16),
        "gate_bias_c": jnp.pad(bias, (0, LANES - 2 * nh)).reshape(1, LANES),
        "gate_bias_r": jnp.pad(bias, (0, rows - 2 * nh)).reshape(rows, 1),
        "g_norm1": g_norm1[l].reshape(1, d), "g_norm2": g_norm2[l].reshape(1, d),
        "g_q": g_q[l].reshape(1, -1), "g_k": g_k[l].reshape(1, -1),
        "w_conv": w_conv[l], "b_conv": b_conv[l].reshape(1, -1),
        "g_h": g_h[l].reshape(nh, 1, -1),
        "w_out": w_out[l].astype(BF16), "w_ff1": w_ff1[l].astype(BF16), "w_ff2": w_ff2[l].astype(BF16),
    }


def kernel(x_prompt, x_sample, c_prompt, c_sample, cache_k, cache_v, state_C, state_n, state_m, state_conv,
           w_ada, b_ada, g_norm1, w_in, g_q, g_k, w_conv, b_conv, b_i, b_f, g_h, w_out, g_norm2, w_ff1, w_ff2):
    depth = w_ada.shape[0]
    bp, bs = c_prompt.shape[0], c_sample.shape[0]
    c_rows = -(-(bp + bs) // (2 * SUBLANES)) * (2 * SUBLANES)
    c_all = jnp.pad(jnp.concatenate([c_prompt, c_sample], axis=0), ((0, c_rows - bp - bs), (0, 0)))
    y_prompt, y_sample = x_prompt, x_sample
    new_p, new_s = [], []
    for l in range(depth):
        wts = _layer_weights(l, w_in, g_norm1, g_q, g_k, w_conv, b_conv, b_i, b_f, g_h, w_out, g_norm2,
                             w_ff1, w_ff2)
        mod = _ada(c_all, w_ada[l], b_ada[l].reshape(1, -1))
        mod_p = mod[:bp].reshape(bp, 1, -1)
        mod_s = mod[bp:bp + bs].reshape(bs, 1, -1)
        y_prompt, sp = _layer(y_prompt, mod_p, None, wts)
        past = (cache_k[l], cache_v[l], state_C[l], state_n[l], state_m[l], state_conv[l])
        y_sample, ss = _layer(y_sample, mod_s, past, wts)
        new_p.append(sp)
        new_s.append(ss)
    stack = lambda states, i: jnp.stack([s[i] for s in states], axis=0)
    return ((y_prompt, y_sample) + tuple(stack(new_p, i) for i in range(6))
            + tuple(stack(new_s, i) for i in range(6)))
```

```python
import functools

import jax
import jax.numpy as jnp
from jax import lax
from jax.experimental import pallas as pl
from jax.experimental.pallas import tpu as pltpu

F32, BF16 = jnp.float32, jnp.bfloat16
EPS = 1e-6
HEAD_DIM_A = 128
N_HEADS_B = 4
CONV_W = 4
LANES = 128
SUBLANES = 8
F32_EXP2_UNDERFLOW = 151.0
LOG2_E = 1.4426950408889634
NO_PAST = 1e30
V7X_VMEM_BYTES = 64 * 2 ** 20
VMEM_LIMIT = 56 * 2 ** 20

ROW_TILE = 512
ATTN_BLOCK = 256
SAMPLE_KEY_BLOCK = 256
MLSTM_CHUNK = 256
FF_TILE = 1024
ADA_TILE = 1024
ATTN_HEADS_PER_STEP = 4
SUFFIX_SUM_TERMS = 1
PROJ_COLS = 256


def _params(semantics, vmem_limit=VMEM_LIMIT):
    return pltpu.CompilerParams(dimension_semantics=semantics, vmem_limit_bytes=vmem_limit)


def _resident(shape, index_map):
    return pl.BlockSpec(shape, index_map, pipeline_mode=pl.Buffered(1))


def _sigmoid(x):
    return 1.0 / (1.0 + jnp.exp(-x))


def _log_sigmoid(x):
    return jnp.minimum(x, 0.0) - jnp.log1p(jnp.exp(-jnp.abs(x)))


def _dot(a, b):
    return jnp.dot(a, b, preferred_element_type=F32)


def _dot_nt(a, b):
    return lax.dot_general(a, b, (((1,), (1,)), ((), ())), preferred_element_type=F32)


def _dot_tn(a, b):
    return lax.dot_general(a, b, (((0,), (0,)), ((), ())), preferred_element_type=F32)


def _split_bf16(x, parts):
    out = []
    for _ in range(parts - 1):
        p = x.astype(BF16)
        out.append(p)
        x = x - p.astype(F32)
    out.append(x.astype(BF16))
    return out


def _modulated_norm(x, g, sc, sh):
    y = x * lax.rsqrt(jnp.mean(x * x, axis=-1, keepdims=True) + EPS)
    return (y * g) * (1.0 + sc) + sh


def _row_blocks(bn, t, rows):
    if t >= rows:
        return 1, rows
    return min(bn, rows // t), t


def _ada_kernel(c_ref, w_ref, b_ref, o_ref):
    c = c_ref[...]
    s = (c * _sigmoid(c)).astype(BF16)
    o_ref[...] = _dot(s, w_ref[...].astype(BF16)) + b_ref[...]


def _ada(c, w, b):
    m, d = c.shape
    n = w.shape[1]
    return pl.pallas_call(
        _ada_kernel,
        out_shape=jax.ShapeDtypeStruct((m, n), F32),
        grid=(n // ADA_TILE,),
        in_specs=[pl.BlockSpec((m, d), lambda j: (0, 0)),
                  pl.BlockSpec((d, ADA_TILE), lambda j: (0, j)),
                  pl.BlockSpec((1, ADA_TILE), lambda j: (0, j))],
        out_specs=pl.BlockSpec((m, ADA_TILE), lambda j: (0, j)),
        compiler_params=_params(("parallel",)),
        name="ada",
    )(c, w, b)


def _proj_attn_kernel(x_ref, sh_ref, sc_ref, g1_ref, w_ref, gq_ref, gk_ref, h_ref, q_ref, k_ref, v_ref, kb_ref,
                      vb_ref):
    bb, tt, d = x_ref.shape
    nh = q_ref.shape[1]
    da = nh * HEAD_DIM_A
    h3 = _modulated_norm(x_ref[...], g1_ref[...], sc_ref[...], sh_ref[...]).astype(BF16)
    h_ref[...] = h3
    hb = h3.reshape(bb * tt, d)
    outs = (((q_ref,), gq_ref), ((k_ref, kb_ref), gk_ref), ((v_ref, vb_ref), None))
    for part, (out_refs, g_ref) in enumerate(outs):
        u = _dot(hb, w_ref[:, part * da:(part + 1) * da])
        for hd in range(nh):
            uh = u[:, hd * HEAD_DIM_A:(hd + 1) * HEAD_DIM_A]
            if g_ref is not None:
                uh = uh * lax.rsqrt(jnp.mean(uh * uh, axis=-1, keepdims=True) + EPS) * g_ref[...]
            for out_ref in out_refs:
                out_ref[:, hd] = uh.reshape(bb, tt, HEAD_DIM_A).astype(out_ref.dtype)


def _proj_attn(x, mod, g1, w, gq, gk):
    bn, t, d = x.shape
    nh = d // 2 // HEAD_DIM_A
    bb, tt = _row_blocks(bn, t, ROW_TILE)
    head_spec = pl.BlockSpec((bb, nh, tt, HEAD_DIM_A), lambda b, i: (b, 0, i, 0))
    return pl.pallas_call(
        _proj_attn_kernel,
        out_shape=(jax.ShapeDtypeStruct((bn, t, d), BF16),) + tuple(
            jax.ShapeDtypeStruct((bn, nh, t, HEAD_DIM_A), dt) for dt in (BF16, F32, F32, BF16, BF16)),
        grid=(bn // bb, t // tt),
        in_specs=[pl.BlockSpec((bb, tt, d), lambda b, i: (b, i, 0)),
                  pl.BlockSpec((bb, 1, d), lambda b, i: (b, 0, 0)),
                  pl.BlockSpec((bb, 1, d), lambda b, i: (b, 0, 1)),
                  pl.BlockSpec((1, d), lambda b, i: (0, 0)),
                  _resident((d, 3 * nh * HEAD_DIM_A), lambda b, i: (0, 0)),
                  pl.BlockSpec((1, HEAD_DIM_A), lambda b, i: (0, 0)),
                  pl.BlockSpec((1, HEAD_DIM_A), lambda b, i: (0, 0))],
        out_specs=(pl.BlockSpec((bb, tt, d), lambda b, i: (b, i, 0)),) + (head_spec,) * 5,
        compiler_params=_params(("parallel", "parallel")),
        name="proj_attn",
    )(x, mod, mod, g1, w, gq, gk)


def _proj_mlstm_kernel(h_ref, wq_ref, wk_ref, wv_ref, wo_ref, wg_ref, wgt_ref, prev_ref,
                       wc_ref, bc_ref, qk_ref, vo_ref, gc_ref, gr_ref, new_ref, halo_scr):
    i = pl.program_id(1)
    bb, tt, d = h_ref.shape
    c = qk_ref.shape[2]
    hb = h_ref[...].reshape(bb * tt, d)

    @pl.when(i == 0)
    def _():
        halo_scr[...] = prev_ref[...]

    db = c // 2
    step = min(PROJ_COLS, db)
    k_scale = (db // N_HEADS_B) ** -0.5

    def w_cols(lo):
        return (wq_ref, wk_ref, wv_ref, wo_ref)[lo // db][:, lo % db:lo % db + step]

    for lo in range(0, c, step):
        cols = slice(lo, lo + step)
        u = _dot(hb, w_cols(lo)).reshape(bb, tt, step)
        ext = jnp.concatenate([halo_scr[:, :, cols], u], axis=1)
        y = pltpu.roll(ext, CONV_W - 1, axis=1)[:, SUBLANES:] * wc_ref[0:1, cols]
        for j in range(1, CONV_W - 1):
            y = y + pltpu.roll(ext, CONV_W - 1 - j, axis=1)[:, SUBLANES:] * wc_ref[j:j + 1, cols]
        y = y + u * wc_ref[CONV_W - 1:CONV_W, cols] + bc_ref[:, cols]
        y = y * _sigmoid(y)
        qk_ref[:, :, cols] = (y * k_scale if lo >= c // 2 else y).astype(qk_ref.dtype)
        halo_scr[:, :, cols] = ext[:, tt:tt + SUBLANES]
    for lo in range(0, c, step):
        vo_ref[:, :, lo:lo + step] = _dot(hb, w_cols(c + lo)).reshape(bb, tt, step).astype(vo_ref.dtype)
    gc_ref[...] = _dot(hb, wg_ref[...]).reshape(bb, tt, LANES)
    for b in range(bb):
        gr_ref[b] = _dot_nt(wgt_ref[...], hb[b * tt:(b + 1) * tt])

    @pl.when(i == pl.num_programs(1) - 1)
    def _():
        new_ref[...] = halo_scr[:, SUBLANES - (CONV_W - 1):SUBLANES]


def _proj_mlstm(h, w, wg, wgt, prev8, wc, bc):
    bn, t, d = h.shape
    db = d - d // 2
    c = 2 * db
    first = (3 * (d // 2)) // db
    bb, tt = _row_blocks(bn, t, ROW_TILE)
    const = lambda a: pl.BlockSpec(a.shape, lambda b, i: (0,) * a.ndim)
    return pl.pallas_call(
        _proj_mlstm_kernel,
        out_shape=(jax.ShapeDtypeStruct((bn, t, c), BF16),
                   jax.ShapeDtypeStruct((bn, t, c), BF16),
                   jax.ShapeDtypeStruct((bn, t, LANES), F32),
                   jax.ShapeDtypeStruct((bn, wgt.shape[0], t), F32),
                   jax.ShapeDtypeStruct((bn, CONV_W - 1, c), F32)),
        grid=(bn // bb, t // tt),
        in_specs=[pl.BlockSpec((bb, tt, d), lambda b, i: (b, i, 0)),
                  *[_resident((d, db), lambda b, i, j=j: (0, first + j)) for j in range(4)],
                  _resident(wg.shape, lambda b, i: (0, 0)),
                  _resident(wgt.shape, lambda b, i: (0, 0)),
                  pl.BlockSpec((bb, SUBLANES, c), lambda b, i: (b, 0, 0)),
                  const(wc), const(bc)],
        out_specs=(pl.BlockSpec((bb, tt, c), lambda b, i: (b, i, 0)),
                   pl.BlockSpec((bb, tt, c), lambda b, i: (b, i, 0)),
                   pl.BlockSpec((bb, tt, LANES), lambda b, i: (b, i, 0)),
                   pl.BlockSpec((bb, wgt.shape[0], tt), lambda b, i: (b, 0, i)),
                   pl.BlockSpec((bb, CONV_W - 1, c), lambda b, i: (b, 0, 0))),
        scratch_shapes=[pltpu.VMEM((bb, SUBLANES, c), F32)],
        compiler_params=_params(("parallel", "arbitrary")),
        name="proj_mlstm",
    )(h, w, w, w, w, wg, wgt, prev8, wc, bc)


def _strict_upper_ones(n):
    r = lax.broadcasted_iota(jnp.int32, (n, n), 0)
    c = lax.broadcasted_iota(jnp.int32, (n, n), 1)
    return (r > c).astype(BF16)


def _sb_block(q, kb, vb, tri, carry, mask):
    z = _dot_nt(q, kb) * (HEAD_DIM_A ** -0.5 * LOG2_E)
    log_beta = jnp.minimum(z, 0.0) - jnp.log2(1.0 + jnp.exp2(-jnp.abs(z)))
    log_1m = log_beta - z
    if mask is not None:
        log_1m = jnp.where(mask, log_1m, 0.0)
    tail = sum(_dot(p, tri) for p in _split_bf16(log_1m, SUFFIX_SUM_TERMS))
    total = tail[:, :1] + log_1m[:, :1]
    if carry is not None:
        tail = tail + carry
    a = jnp.exp2(log_beta + tail)
    if mask is not None:
        a = jnp.where(mask, a, 0.0)
    return _dot(a.astype(BF16), vb), total


def _attn_kernel(q_ref, kd_ref, vd_ref, kp_ref, vp_ref, o_ref, carry_ref, *, tk, past_is_prefix):
    hps, tq, dh = q_ref.shape[1], q_ref.shape[2], q_ref.shape[3]
    row = lax.broadcasted_iota(jnp.int32, (tq, tq), 0)
    col = lax.broadcasted_iota(jnp.int32, (tq, tq), 1)
    n_past = pl.program_id(2) * (tq // tk) if past_is_prefix else kp_ref.shape[2] // tk
    tri = _strict_upper_ones(tk)
    tri_diag = tri if tq == tk else _strict_upper_ones(tq)

    def past_block(h, jj, carry):
        start = pl.multiple_of(jnp.maximum(n_past - 1 - jj, 0) * tk, tk)
        kb = kp_ref[0, h, pl.ds(start, tk), :].astype(BF16)
        vb = vp_ref[0, h, pl.ds(start, tk), :].astype(BF16)
        return _sb_block(q_ref[0, h], kb, vb, tri, carry, None)

    state = []
    for h in range(hps):
        acc, s = _sb_block(q_ref[0, h], kd_ref[0, h].astype(BF16), vd_ref[0, h].astype(BF16), tri_diag, None,
                           col < row)
        pv, ds = past_block(h, 0, jnp.where(n_past > 0, s, -NO_PAST))
        state.append((acc + pv, s + ds))

    worst = None
    for h, (acc, s) in enumerate(state):
        def more(c):
            jj, _, s = c
            return jnp.logical_and(jj < n_past, jnp.max(s) > -F32_EXP2_UNDERFLOW)

        def body(c, h=h):
            jj, acc, s = c
            pv, ds = past_block(h, jj, s)
            return jj + 1, acc + pv, s + ds

        _, acc, s = lax.while_loop(more, body, (jnp.int32(1), acc, s))
        o_ref[0, :, h * dh:(h + 1) * dh] = acc.astype(o_ref.dtype)
        top = jnp.max(s, axis=0, keepdims=True)
        worst = top if worst is None else jnp.maximum(worst, top)
    carry_ref[0, 0, 0] = jnp.broadcast_to(worst, carry_ref.shape[3:])


def _attn_call(q, k, v, k_past, v_past, prefix, window):
    bn, nh, t, dh = q.shape
    if prefix:
        tq = tk = min(t, ATTN_BLOCK)
    else:
        tq, tk = t, min(k_past.shape[2], SAMPLE_KEY_BLOCK)
    p = k_past.shape[2] if window is None else window
    last = k_past.shape[2] // p - 1
    hps = min(nh, ATTN_HEADS_PER_STEP)
    blk = pl.BlockSpec((1, hps, tq, dh), lambda b, h, i: (b, h, i, 0))
    past = pl.BlockSpec((1, hps, p, dh), lambda b, h, i: (b, h, last, 0))
    return pl.pallas_call(
        functools.partial(_attn_kernel, tk=tk, past_is_prefix=prefix),
        out_shape=(jax.ShapeDtypeStruct((bn, t, nh * dh), BF16),
                   jax.ShapeDtypeStruct((bn, nh // hps, t // tq, SUBLANES, LANES), F32)),
        grid=(bn, nh // hps, t // tq),
        in_specs=[blk, blk, blk, past, past],
        out_specs=(pl.BlockSpec((1, tq, hps * dh), lambda b, h, i: (b, i, h)),
                   pl.BlockSpec((1, 1, 1, SUBLANES, LANES), lambda b, h, i: (b, h, i, 0, 0))),
        compiler_params=_params(("parallel", "parallel", "parallel")),
        name="attn",
    )(q, k, v, k_past, v_past)


def _attn(q, k, v, k_past, v_past):
    if k_past is None:
        return _attn_call(q, k, v, k, v, True, None)[0]
    near, carry = _attn_call(q, k, v, k_past, v_past, False, min(k_past.shape[2], SAMPLE_KEY_BLOCK))
    return lax.cond(jnp.max(carry) > -F32_EXP2_UNDERFLOW,
                    lambda: _attn_call(q, k, v, k_past, v_past, False, None)[0],
                    lambda: near)


def _mlstm_kernel(qk_ref, vo_ref, gc_ref, gr_ref, bc_ref, br_ref, gh_ref, c0_ref, n0_ref, m0_ref,
                  o_ref, c_out_ref, n_out_ref, m_out_ref, c_scr, n_scr, m_scr):
    ci = pl.program_id(1)
    ln = qk_ref.shape[1]
    nh = N_HEADS_B
    dh = qk_ref.shape[2] // (2 * nh)

    @pl.when(ci == 0)
    def _():
        c_scr[...] = c0_ref[0]
        n_scr[...] = n0_ref[0]
        m_scr[...] = m0_ref[0]

    gc = gc_ref[0] + bc_ref[...]
    gr = gr_ref[0] + br_ref[...]
    lfc = _log_sigmoid(gc)
    lfr = _log_sigmoid(gr)
    row = lax.broadcasted_iota(jnp.int32, (ln, ln), 0)
    col = lax.broadcasted_iota(jnp.int32, (ln, ln), 1)
    causal = col <= row
    ones_causal = causal.astype(BF16)
    ones_causal_t = (row <= col).astype(BF16)
    b_rows = sum(_dot(p, ones_causal_t) for p in _split_bf16(lfr, 3))

    for hd in range(nh):
        q = qk_ref[0, :, hd * dh:(hd + 1) * dh]
        k = qk_ref[0, :, (nh + hd) * dh:(nh + hd + 1) * dh]
        v = vo_ref[0, :, hd * dh:(hd + 1) * dh]
        og = vo_ref[0, :, (nh + hd) * dh:(nh + hd + 1) * dh].astype(F32)
        ig_col = gc[:, hd:hd + 1]
        ig_row = gr[hd:hd + 1, :]
        b_row = b_rows[nh + hd:nh + hd + 1, :]
        lf_b = jnp.broadcast_to(lfc[:, nh + hd:nh + hd + 1], (ln, ln))
        bt = sum(_dot(ones_causal, p) for p in _split_bf16(lf_b, 3))
        b_col = bt[:, :1]
        m_prev = m_scr[hd][:, :1]
        c_prev = c_scr[hd]
        n_prev = n_scr[hd]

        dmat = jnp.where(causal, (bt - b_row) + ig_row, -jnp.inf)
        inter = b_col + m_prev
        m_t = jnp.maximum(inter, jnp.max(dmat, axis=1, keepdims=True))
        w_inter = jnp.exp(inter - m_t)
        s = jnp.exp(dmat - m_t) * _dot_nt(q, k)
        num = w_inter * _dot(q, c_prev.astype(BF16)) + _dot(s.astype(BF16), v)
        den = (w_inter * jnp.sum(q.astype(F32) * n_prev, axis=1, keepdims=True)
               + jnp.sum(s, axis=1, keepdims=True))
        h = num / jnp.maximum(jnp.abs(den), jnp.exp(-m_t))
        hn = h * lax.rsqrt(jnp.mean(h * h, axis=1, keepdims=True) + EPS) * gh_ref[hd]
        o_ref[0, :, hd * dh:(hd + 1) * dh] = (hn * _sigmoid(og)).astype(o_ref.dtype)

        b_last = b_col[ln - 1:ln]
        m_new = m_t[ln - 1:ln]
        w_state = jnp.exp((b_last - b_col) + ig_col - m_new)
        decay = jnp.exp(b_last + m_prev - m_new)
        ks = k.astype(F32) * w_state
        c_scr[hd] = decay * c_prev + _dot_tn(ks.astype(BF16), v)
        n_scr[hd] = decay * n_prev + jnp.sum(ks, axis=0, keepdims=True)
        m_scr[hd] = jnp.broadcast_to(m_new, (1, LANES))

    @pl.when(ci == pl.num_programs(1) - 1)
    def _():
        c_out_ref[0] = c_scr[...]
        n_out_ref[0] = n_scr[...]
        m_out_ref[0] = m_scr[...]


def _mlstm(qk, vo, gc, gr, bias_c, bias_r, gh, c0, n0, m0):
    bn, t, c2 = qk.shape
    nh = N_HEADS_B
    dh = c2 // (2 * nh)
    ln = min(t, MLSTM_CHUNK)
    rows = gr.shape[1]
    state = lambda shape: pl.BlockSpec((1,) + shape, lambda b, i: (b, 0, 0, 0))
    return pl.pallas_call(
        _mlstm_kernel,
        out_shape=(jax.ShapeDtypeStruct((bn, t, nh * dh), BF16),
                   jax.ShapeDtypeStruct((bn, nh, dh, dh), F32),
                   jax.ShapeDtypeStruct((bn, nh, 1, dh), F32),
                   jax.ShapeDtypeStruct((bn, nh, 1, LANES), F32)),
        grid=(bn, t // ln),
        in_specs=[pl.BlockSpec((1, ln, c2), lambda b, i: (b, i, 0)),
                  pl.BlockSpec((1, ln, c2), lambda b, i: (b, i, 0)),
                  pl.BlockSpec((1, ln, LANES), lambda b, i: (b, i, 0)),
                  pl.BlockSpec((1, rows, ln), lambda b, i: (b, 0, i)),
                  pl.BlockSpec((1, LANES), lambda b, i: (0, 0)),
                  pl.BlockSpec((rows, 1), lambda b, i: (0, 0)),
                  pl.BlockSpec((nh, 1, dh), lambda b, i: (0, 0, 0)),
                  state((nh, dh, dh)), state((nh, 1, dh)), state((nh, 1, LANES))],
        out_specs=(pl.BlockSpec((1, ln, nh * dh), lambda b, i: (b, i, 0)),
                   state((nh, dh, dh)), state((nh, 1, dh)), state((nh, 1, LANES))),
        scratch_shapes=[pltpu.VMEM((nh, dh, dh), F32), pltpu.VMEM((nh, 1, dh), F32),
                        pltpu.VMEM((nh, 1, LANES), F32)],
        compiler_params=_params(("parallel", "arbitrary")),
        name="mlstm",
    )(qk, vo, gc, gr, bias_c, bias_r, gh, c0, n0, m0)


def _out_proj_kernel(x_ref, oa_ref, ob_ref, gt_ref, w_ref, sh_ref, sc_ref, g2_ref, o_ref, h_ref):
    bb, tt, d = x_ref.shape
    da, db = oa_ref.shape[2], ob_ref.shape[2]
    y = (_dot(oa_ref[...].reshape(bb * tt, da), w_ref[0:da])
         + _dot(ob_ref[...].reshape(bb * tt, db), w_ref[da:da + db]))
    x1 = x_ref[...] + gt_ref[...] * y.reshape(bb, tt, d)
    o_ref[...] = x1
    h_ref[...] = _modulated_norm(x1, g2_ref[...], sc_ref[...], sh_ref[...]).astype(BF16)


def _out_proj(x, oa, ob, mod, w, g2):
    bn, t, d = x.shape
    bb, tt = _row_blocks(bn, t, ROW_TILE)
    rows = lambda width: pl.BlockSpec((bb, tt, width), lambda b, i: (b, i, 0))
    return pl.pallas_call(
        _out_proj_kernel,
        out_shape=(jax.ShapeDtypeStruct((bn, t, d), F32), jax.ShapeDtypeStruct((bn, t, d), BF16)),
        grid=(bn // bb, t // tt),
        in_specs=[rows(d), rows(oa.shape[2]), rows(ob.shape[2]),
                  pl.BlockSpec((bb, 1, d), lambda b, i: (b, 0, 2)),
                  _resident(w.shape, lambda b, i: (0, 0)),
                  pl.BlockSpec((bb, 1, d), lambda b, i: (b, 0, 3)),
                  pl.BlockSpec((bb, 1, d), lambda b, i: (b, 0, 4)),
                  pl.BlockSpec((1, d), lambda b, i: (0, 0))],
        out_specs=(rows(d), rows(d)),
        compiler_params=_params(("parallel", "parallel")),
        name="out_proj",
    )(x, oa, ob, mod, w, mod, mod, g2)


def _ffn_kernel(x_ref, h_ref, gt_ref, w1_ref, w2_ref, o_ref, acc_scr):
    f = pl.program_id(2)
    bb, tt, d = x_ref.shape

    @pl.when(f == 0)
    def _():
        acc_scr[...] = jnp.zeros_like(acc_scr)

    a = jnp.maximum(_dot(h_ref[...].reshape(bb * tt, d), w1_ref[...]), 0.0)
    acc_scr[...] += _dot((a * a).astype(BF16), w2_ref[...])

    @pl.when(f == pl.num_programs(2) - 1)
    def _():
        o_ref[...] = x_ref[...] + gt_ref[...] * acc_scr[...].reshape(bb, tt, d)


def _ffn(x, h, mod, w1, w2):
    bn, t, d = x.shape
    dff = w1.shape[1]
    bb, tt = _row_blocks(bn, t, ROW_TILE)
    rows = pl.BlockSpec((bb, tt, d), lambda b, i, f: (b, i, 0))
    chunk = lambda c: pl.BlockSpec((bb, 1, d), lambda b, i, f: (b, 0, c))
    return pl.pallas_call(
        _ffn_kernel,
        out_shape=jax.ShapeDtypeStruct((bn, t, d), F32),
        grid=(bn // bb, t // tt, dff // FF_TILE),
        in_specs=[rows, rows, chunk(5),
                  pl.BlockSpec((d, FF_TILE), lambda b, i, f: (0, f)),
                  pl.BlockSpec((FF_TILE, d), lambda b, i, f: (f, 0))],
        out_specs=rows,
        scratch_shapes=[pltpu.VMEM((bb * tt, d), F32)],
        compiler_params=_params(("parallel", "parallel", "arbitrary")),
        name="ffn",
    )(x, h, mod, w1, w2)


def _layer(x, mod, past, wts):
    bn, t, d = x.shape
    db = d - d // 2
    nh, dh = N_HEADS_B, db // N_HEADS_B
    if past is None:
        k_past = v_past = None
        conv_prev = jnp.zeros((bn, CONV_W - 1, 2 * db), F32)
        c0 = jnp.zeros((bn, nh, dh, dh), F32)
        n0 = jnp.zeros((bn, nh, dh), F32)
        m0 = jnp.zeros((bn, nh), F32)
    else:
        k_past, v_past, c0, n0, m0, conv_prev = (a.astype(F32) for a in past)

    h, q, k, v, k_bf, v_bf = _proj_attn(x, mod, wts["g_norm1"], wts["w_in"], wts["g_q"], wts["g_k"])
    prev8 = jnp.pad(conv_prev, ((0, 0), (SUBLANES - (CONV_W - 1), 0), (0, 0)))
    qk, vo, gc, gr, conv_new = _proj_mlstm(h, wts["w_in"], wts["w_gate"], wts["w_gate_t"], prev8, wts["w_conv"],
                                           wts["b_conv"])
    o_a = _attn(q, k_bf, v_bf, k_past, v_past)
    o_b, c_new, n_new, m_new = _mlstm(
        qk, vo, gc, gr, wts["gate_bias_c"], wts["gate_bias_r"], wts["g_h"], c0, n0.reshape(bn, nh, 1, dh),
        jnp.broadcast_to(m0[:, :, None, None], (bn, nh, 1, LANES)))

    x, h2 = _out_proj(x, o_a, o_b, mod, wts["w_out"], wts["g_norm2"])
    x = _ffn(x, h2, mod, wts["w_ff1"], wts["w_ff2"])
    return x, (k, v, c_new, n_new.reshape(bn, nh, dh), m_new[:, :, 0, 0], conv_new)


def _layer_weights(l, w_in, g_norm1, g_q, g_k, w_conv, b_conv, b_i, b_f, g_h, w_out, g_norm2, w_ff1, w_ff2):
    d = w_in.shape[1]
    nh = N_HEADS_B
    w_gate = w_in[l][:, w_in.shape[2] - 2 * nh:]
    bias = jnp.concatenate([b_i[l], b_f[l]])
    rows = 2 * SUBLANES
    return {
        "w_in": w_in[l].astype(BF16),
        "w_gate": jnp.pad(w_gate, ((0, 0), (0, LANES - 2 * nh))).astype(BF16),
        "w_gate_t": jnp.pad(w_gate.T, ((0, rows - 2 * nh), (0, 0))).astype(BF16),
        "gate_bias_c": jnp.pad(bias, (0, LANES - 2 * nh)).reshape(1, LANES),
        "gate_bias_r": jnp.pad(bias, (0, rows - 2 * nh)).reshape(rows, 1),
        "g_norm1": g_norm1[l].reshape(1, d), "g_norm2": g_norm2[l].reshape(1, d),
        "g_q": g_q[l].reshape(1, -1), "g_k": g_k[l].reshape(1, -1),
        "w_conv": w_conv[l], "b_conv": b_conv[l].reshape(1, -1),
        "g_h": g_h[l].reshape(nh, 1, -1),
        "w_out": w_out[l].astype(BF16), "w_ff1": w_ff1[l].astype(BF16), "w_ff2": w_ff2[l].astype(BF16),
    }


def kernel(x_prompt, x_sample, c_prompt, c_sample, cache_k, cache_v, state_C, state_n, state_m, state_conv,
           w_ada, b_ada, g_norm1, w_in, g_q, g_k, w_conv, b_conv, b_i, b_f, g_h, w_out, g_norm2, w_ff1, w_ff2):
    depth = w_ada.shape[0]
    bp, bs = c_prompt.shape[0], c_sample.shape[0]
    c_rows = -(-(bp + bs) // (2 * SUBLANES)) * (2 * SUBLANES)
    c_all = jnp.pad(jnp.concatenate([c_prompt, c_sample], axis=0), ((0, c_rows - bp - bs), (0, 0)))
    y_prompt, y_sample = x_prompt, x_sample
    new_p, new_s = [], []
    for l in range(depth):
        wts = _layer_weights(l, w_in, g_norm1, g_q, g_k, w_conv, b_conv, b_i, b_f, g_h, w_out, g_norm2,
                             w_ff1, w_ff2)
        mod = _ada(c_all, w_ada[l], b_ada[l].reshape(1, -1))
        mod_p = mod[:bp].reshape(bp, 1, -1)
        mod_s = mod[bp:bp + bs].reshape(bs, 1, -1)
        y_prompt, sp = _layer(y_prompt, mod_p, None, wts)
        past = (cache_k[l], cache_v[l], state_C[l], state_n[l], state_m[l], state_conv[l])
        y_sample, ss = _layer(y_sample, mod_s, past, wts)
        new_p.append(sp)
        new_s.append(ss)
    stack = lambda states, i: jnp.stack([s[i] for s in states], axis=0)
    return ((y_prompt, y_sample) + tuple(stack(new_p, i) for i in range(6))
            + tuple(stack(new_s, i) for i in range(6)))
```

```python
import functools

import jax
import jax.numpy as jnp
from jax import lax
from jax.experimental import pallas as pl
from jax.experimental.pallas import tpu as pltpu

F32, BF16 = jnp.float32, jnp.bfloat16
EPS = 1e-6
HEAD_DIM_A = 128
N_HEADS_B = 4
CONV_W = 4
LANES = 128
SUBLANES = 8
F32_EXP2_UNDERFLOW = 151.0
LOG2_E = 1.4426950408889634
NO_PAST = 1e30
V7X_VMEM_BYTES = 64 * 2 ** 20
VMEM_LIMIT = V7X_VMEM_BYTES * 7 // 8

ROW_TILE = 512
ATTN_BLOCK = 256
SAMPLE_KEY_BLOCK = 256
MLSTM_CHUNK = 256
FF_TILE = 1024
ADA_TILE = 1024
ATTN_HEADS_PER_STEP = 4
SUFFIX_SUM_TERMS = 1
PROJ_COLS = 256


def _params(semantics, vmem_limit=VMEM_LIMIT):
    return pltpu.CompilerParams(dimension_semantics=semantics, vmem_limit_bytes=vmem_limit)


def _resident(shape, index_map):
    return pl.BlockSpec(shape, index_map, pipeline_mode=pl.Buffered(1))


def _sigmoid(x):
    return 1.0 / (1.0 + jnp.exp(-x))


def _log_sigmoid(x):
    return jnp.minimum(x, 0.0) - jnp.log1p(jnp.exp(-jnp.abs(x)))


def _dot(a, b):
    return jnp.dot(a, b, preferred_element_type=F32)


def _dot_nt(a, b):
    return lax.dot_general(a, b, (((1,), (1,)), ((), ())), preferred_element_type=F32)


def _dot_tn(a, b):
    return lax.dot_general(a, b, (((0,), (0,)), ((), ())), preferred_element_type=F32)


def _split_bf16(x, parts):
    out = []
    for _ in range(parts - 1):
        p = x.astype(BF16)
        out.append(p)
        x = x - p.astype(F32)
    out.append(x.astype(BF16))
    return out


def _modulated_norm(x, g, sc, sh):
    y = x * lax.rsqrt(jnp.mean(x * x, axis=-1, keepdims=True) + EPS)
    return (y * g) * (1.0 + sc) + sh


def _row_blocks(bn, t, rows):
    if t >= rows:
        return 1, rows
    return min(bn, rows // t), t


def _ada_kernel(c_ref, w_ref, b_ref, o_ref):
    c = c_ref[...]
    s = (c * _sigmoid(c)).astype(BF16)
    o_ref[...] = _dot(s, w_ref[...].astype(BF16)) + b_ref[...]


def _ada(c, w, b):
    m, d = c.shape
    n = w.shape[1]
    return pl.pallas_call(
        _ada_kernel,
        out_shape=jax.ShapeDtypeStruct((m, n), F32),
        grid=(n // ADA_TILE,),
        in_specs=[pl.BlockSpec((m, d), lambda j: (0, 0)),
                  pl.BlockSpec((d, ADA_TILE), lambda j: (0, j)),
                  pl.BlockSpec((1, ADA_TILE), lambda j: (0, j))],
        out_specs=pl.BlockSpec((m, ADA_TILE), lambda j: (0, j)),
        compiler_params=_params(("parallel",)),
        name="ada",
    )(c, w, b)


def _proj_attn_kernel(x_ref, sh_ref, sc_ref, g1_ref, w_ref, gq_ref, gk_ref, h_ref, q_ref, k_ref, v_ref, kb_ref,
                      vb_ref):
    bb, tt, d = x_ref.shape
    nh = q_ref.shape[1]
    da = nh * HEAD_DIM_A
    h3 = _modulated_norm(x_ref[...], g1_ref[...], sc_ref[...], sh_ref[...]).astype(BF16)
    h_ref[...] = h3
    hb = h3.reshape(bb * tt, d)
    outs = (((q_ref,), gq_ref), ((k_ref, kb_ref), gk_ref), ((v_ref, vb_ref), None))
    for part, (out_refs, g_ref) in enumerate(outs):
        u = _dot(hb, w_ref[:, part * da:(part + 1) * da])
        for hd in range(nh):
            uh = u[:, hd * HEAD_DIM_A:(hd + 1) * HEAD_DIM_A]
            if g_ref is not None:
                uh = uh * lax.rsqrt(jnp.mean(uh * uh, axis=-1, keepdims=True) + EPS) * g_ref[...]
            for out_ref in out_refs:
                out_ref[:, hd] = uh.reshape(bb, tt, HEAD_DIM_A).astype(out_ref.dtype)


def _proj_attn(x, mod, g1, w, gq, gk):
    bn, t, d = x.shape
    nh = d // 2 // HEAD_DIM_A
    bb, tt = _row_blocks(bn, t, ROW_TILE)
    head_spec = pl.BlockSpec((bb, nh, tt, HEAD_DIM_A), lambda b, i: (b, 0, i, 0))
    return pl.pallas_call(
        _proj_attn_kernel,
        out_shape=(jax.ShapeDtypeStruct((bn, t, d), BF16),) + tuple(
            jax.ShapeDtypeStruct((bn, nh, t, HEAD_DIM_A), dt) for dt in (BF16, F32, F32, BF16, BF16)),
        grid=(bn // bb, t // tt),
        in_specs=[pl.BlockSpec((bb, tt, d), lambda b, i: (b, i, 0)),
                  pl.BlockSpec((bb, 1, d), lambda b, i: (b, 0, 0)),
                  pl.BlockSpec((bb, 1, d), lambda b, i: (b, 0, 1)),
                  pl.BlockSpec((1, d), lambda b, i: (0, 0)),
                  _resident((d, 3 * nh * HEAD_DIM_A), lambda b, i: (0, 0)),
                  pl.BlockSpec((1, HEAD_DIM_A), lambda b, i: (0, 0)),
                  pl.BlockSpec((1, HEAD_DIM_A), lambda b, i: (0, 0))],
        out_specs=(pl.BlockSpec((bb, tt, d), lambda b, i: (b, i, 0)),) + (head_spec,) * 5,
        compiler_params=_params(("parallel", "parallel")),
        name="proj_attn",
    )(x, mod, mod, g1, w, gq, gk)


def _proj_mlstm_kernel(h_ref, wq_ref, wk_ref, wv_ref, wo_ref, wg_ref, wgt_ref, prev_ref,
                       wc_ref, bc_ref, qk_ref, vo_ref, gc_ref, gr_ref, new_ref, halo_scr):
    i = pl.program_id(1)
    bb, tt, d = h_ref.shape
    c = qk_ref.shape[2]
    hb = h_ref[...].reshape(bb * tt, d)

    @pl.when(i == 0)
    def _():
        halo_scr[...] = prev_ref[...]

    db = c // 2
    step = min(PROJ_COLS, db)
    k_scale = (db // N_HEADS_B) ** -0.5

    def w_cols(lo):
        return (wq_ref, wk_ref, wv_ref, wo_ref)[lo // db][:, lo % db:lo % db + step]

    for lo in range(0, c, step):
        cols = slice(lo, lo + step)
        u = _dot(hb, w_cols(lo)).reshape(bb, tt, step)
        ext = jnp.concatenate([halo_scr[:, :, cols], u], axis=1)
        y = pltpu.roll(ext, CONV_W - 1, axis=1)[:, SUBLANES:] * wc_ref[0:1, cols]
        for j in range(1, CONV_W - 1):
            y = y + pltpu.roll(ext, CONV_W - 1 - j, axis=1)[:, SUBLANES:] * wc_ref[j:j + 1, cols]
        y = y + u * wc_ref[CONV_W - 1:CONV_W, cols] + bc_ref[:, cols]
        y = y * _sigmoid(y)
        qk_ref[:, :, cols] = (y * k_scale if lo >= c // 2 else y).astype(qk_ref.dtype)
        halo_scr[:, :, cols] = ext[:, tt:tt + SUBLANES]
    for lo in range(0, c, step):
        vo_ref[:, :, lo:lo + step] = _dot(hb, w_cols(c + lo)).reshape(bb, tt, step).astype(vo_ref.dtype)
    gc_ref[...] = _dot(hb, wg_ref[...]).reshape(bb, tt, LANES)
    for b in range(bb):
        gr_ref[b] = _dot_nt(wgt_ref[...], hb[b * tt:(b + 1) * tt])

    @pl.when(i == pl.num_programs(1) - 1)
    def _():
        new_ref[...] = halo_scr[:, SUBLANES - (CONV_W - 1):SUBLANES]


def _proj_mlstm(h, w, wg, wgt, prev8, wc, bc):
    bn, t, d = h.shape
    db = d - d // 2
    c = 2 * db
    first = (3 * (d // 2)) // db
    bb, tt = _row_blocks(bn, t, ROW_TILE)
    const = lambda a: pl.BlockSpec(a.shape, lambda b, i: (0,) * a.ndim)
    return pl.pallas_call(
        _proj_mlstm_kernel,
        out_shape=(jax.ShapeDtypeStruct((bn, t, c), BF16),
                   jax.ShapeDtypeStruct((bn, t, c), BF16),
                   jax.ShapeDtypeStruct((bn, t, LANES), F32),
                   jax.ShapeDtypeStruct((bn, wgt.shape[0], t), F32),
                   jax.ShapeDtypeStruct((bn, CONV_W - 1, c), F32)),
        grid=(bn // bb, t // tt),
        in_specs=[pl.BlockSpec((bb, tt, d), lambda b, i: (b, i, 0)),
                  *[_resident((d, db), lambda b, i, j=j: (0, first + j)) for j in range(4)],
                  _resident(wg.shape, lambda b, i: (0, 0)),
                  _resident(wgt.shape, lambda b, i: (0, 0)),
                  pl.BlockSpec((bb, SUBLANES, c), lambda b, i: (b, 0, 0)),
                  const(wc), const(bc)],
        out_specs=(pl.BlockSpec((bb, tt, c), lambda b, i: (b, i, 0)),
                   pl.BlockSpec((bb, tt, c), lambda b, i: (b, i, 0)),
                   pl.BlockSpec((bb, tt, LANES), lambda b, i: (b, i, 0)),
                   pl.BlockSpec((bb, wgt.shape[0], tt), lambda b, i: (b, 0, i)),
                   pl.BlockSpec((bb, CONV_W - 1, c), lambda b, i: (b, 0, 0))),
        scratch_shapes=[pltpu.VMEM((bb, SUBLANES, c), F32)],
        compiler_params=_params(("parallel", "arbitrary")),
        name="proj_mlstm",
    )(h, w, w, w, w, wg, wgt, prev8, wc, bc)


def _strict_upper_ones(n):
    r = lax.broadcasted_iota(jnp.int32, (n, n), 0)
    c = lax.broadcasted_iota(jnp.int32, (n, n), 1)
    return (r > c).astype(BF16)


def _sb_block(q, kb, vb, tri, carry, mask):
    z = _dot_nt(q, kb) * (HEAD_DIM_A ** -0.5 * LOG2_E)
    log_beta = jnp.minimum(z, 0.0) - jnp.log2(1.0 + jnp.exp2(-jnp.abs(z)))
    log_1m = log_beta - z
    if mask is not None:
        log_1m = jnp.where(mask, log_1m, 0.0)
    tail = sum(_dot(p, tri) for p in _split_bf16(log_1m, SUFFIX_SUM_TERMS))
    total = tail[:, :1] + log_1m[:, :1]
    if carry is not None:
        tail = tail + carry
    a = jnp.exp2(log_beta + tail)
    if mask is not None:
        a = jnp.where(mask, a, 0.0)
    return _dot(a.astype(BF16), vb), total


def _attn_kernel(q_ref, kd_ref, vd_ref, kp_ref, vp_ref, o_ref, *maybe_carry_ref, tk, past_is_prefix):
    hps, tq, dh = q_ref.shape[1], q_ref.shape[2], q_ref.shape[3]
    row = lax.broadcasted_iota(jnp.int32, (tq, tq), 0)
    col = lax.broadcasted_iota(jnp.int32, (tq, tq), 1)
    n_past = pl.program_id(2) * (tq // tk) if past_is_prefix else kp_ref.shape[2] // tk
    tri = _strict_upper_ones(tk)
    tri_diag = tri if tq == tk else _strict_upper_ones(tq)

    def past_block(h, jj, carry):
        start = pl.multiple_of(jnp.maximum(n_past - 1 - jj, 0) * tk, tk)
        kb = kp_ref[0, h, pl.ds(start, tk), :].astype(BF16)
        vb = vp_ref[0, h, pl.ds(start, tk), :].astype(BF16)
        return _sb_block(q_ref[0, h], kb, vb, tri, carry, None)

    state = []
    for h in range(hps):
        acc, s = _sb_block(q_ref[0, h], kd_ref[0, h].astype(BF16), vd_ref[0, h].astype(BF16), tri_diag, None,
                           col < row)
        pv, ds = past_block(h, 0, jnp.where(n_past > 0, s, -NO_PAST))
        state.append((acc + pv, s + ds))

    worst = None
    for h, (acc, s) in enumerate(state):
        def more(c):
            jj, _, s = c
            return jnp.logical_and(jj < n_past, jnp.max(s) > -F32_EXP2_UNDERFLOW)

        def body(c, h=h):
            jj, acc, s = c
            pv, ds = past_block(h, jj, s)
            return jj + 1, acc + pv, s + ds

        _, acc, s = lax.while_loop(more, body, (jnp.int32(1), acc, s))
        o_ref[0, :, h * dh:(h + 1) * dh] = acc.astype(o_ref.dtype)
        if maybe_carry_ref:
            top = jnp.max(s, axis=0, keepdims=True)
            worst = top if worst is None else jnp.maximum(worst, top)
    if maybe_carry_ref:
        carry_ref, = maybe_carry_ref
        carry_ref[0, 0, 0] = jnp.broadcast_to(worst, carry_ref.shape[3:])


def _attn_call(q, k, v, k_past, v_past, prefix, window):
    bn, nh, t, dh = q.shape
    if prefix:
        tq = tk = min(t, ATTN_BLOCK)
    else:
        tq, tk = t, min(k_past.shape[2], SAMPLE_KEY_BLOCK)
    p = k_past.shape[2] if window is None else window
    last = k_past.shape[2] // p - 1
    hps = min(nh, ATTN_HEADS_PER_STEP)
    blk = pl.BlockSpec((1, hps, tq, dh), lambda b, h, i: (b, h, i, 0))
    past = pl.BlockSpec((1, hps, p, dh), lambda b, h, i: (b, h, last, 0))
    out_shape = [jax.ShapeDtypeStruct((bn, t, nh * dh), BF16)]
    out_specs = [pl.BlockSpec((1, tq, hps * dh), lambda b, h, i: (b, i, h))]
    if window is not None:
        out_shape.append(jax.ShapeDtypeStruct((bn, nh // hps, t // tq, SUBLANES, LANES), F32))
        out_specs.append(pl.BlockSpec((1, 1, 1, SUBLANES, LANES), lambda b, h, i: (b, h, i, 0, 0)))
    return pl.pallas_call(
        functools.partial(_attn_kernel, tk=tk, past_is_prefix=prefix),
        out_shape=tuple(out_shape),
        grid=(bn, nh // hps, t // tq),
        in_specs=[blk, blk, blk, past, past],
        out_specs=tuple(out_specs),
        compiler_params=_params(("parallel", "parallel", "parallel")),
        name="attn",
    )(q, k, v, k_past, v_past)


def _attn(q, k, v, k_past, v_past):
    if k_past is None:
        return _attn_call(q, k, v, k, v, True, None)[0]
    near, carry = _attn_call(q, k, v, k_past, v_past, False, min(k_past.shape[2], SAMPLE_KEY_BLOCK))
    return lax.cond(jnp.max(carry) > -F32_EXP2_UNDERFLOW,
                    lambda: _attn_call(q, k, v, k_past, v_past, False, None)[0],
                    lambda: near)


def _mlstm_kernel(qk_ref, vo_ref, gc_ref, gr_ref, bc_ref, br_ref, gh_ref, c0_ref, n0_ref, m0_ref,
                  o_ref, c_out_ref, n_out_ref, m_out_ref, c_scr, n_scr, m_scr):
    ci = pl.program_id(1)
    ln = qk_ref.shape[1]
    nh = N_HEADS_B
    dh = qk_ref.shape[2] // (2 * nh)

    @pl.when(ci == 0)
    def _():
        c_scr[...] = c0_ref[0]
        n_scr[...] = n0_ref[0]
        m_scr[...] = m0_ref[0]

    gc = gc_ref[0] + bc_ref[...]
    gr = gr_ref[0] + br_ref[...]
    lfc = _log_sigmoid(gc)
    lfr = _log_sigmoid(gr)
    row = lax.broadcasted_iota(jnp.int32, (ln, ln), 0)
    col = lax.broadcasted_iota(jnp.int32, (ln, ln), 1)
    causal = col <= row
    ones_causal = causal.astype(BF16)
    ones_causal_t = (row <= col).astype(BF16)
    b_rows = sum(_dot(p, ones_causal_t) for p in _split_bf16(lfr, 3))

    for hd in range(nh):
        q = qk_ref[0, :, hd * dh:(hd + 1) * dh]
        k = qk_ref[0, :, (nh + hd) * dh:(nh + hd + 1) * dh]
        v = vo_ref[0, :, hd * dh:(hd + 1) * dh]
        og = vo_ref[0, :, (nh + hd) * dh:(nh + hd + 1) * dh].astype(F32)
        ig_col = gc[:, hd:hd + 1]
        ig_row = gr[hd:hd + 1, :]
        b_row = b_rows[nh + hd:nh + hd + 1, :]
        lf_b = jnp.broadcast_to(lfc[:, nh + hd:nh + hd + 1], (ln, ln))
        bt = sum(_dot(ones_causal, p) for p in _split_bf16(lf_b, 3))
        b_col = bt[:, :1]
        m_prev = m_scr[hd][:, :1]
        c_prev = c_scr[hd]
        n_prev = n_scr[hd]

        dmat = jnp.where(causal, (bt - b_row) + ig_row, -jnp.inf)
        inter = b_col + m_prev
        m_t = jnp.maximum(inter, jnp.max(dmat, axis=1, keepdims=True))
        w_inter = jnp.exp(inter - m_t)
        s = jnp.exp(dmat - m_t) * _dot_nt(q, k)
        num = w_inter * _dot(q, c_prev.astype(BF16)) + _dot(s.astype(BF16), v)
        den = (w_inter * jnp.sum(q.astype(F32) * n_prev, axis=1, keepdims=True)
               + jnp.sum(s, axis=1, keepdims=True))
        h = num / jnp.maximum(jnp.abs(den), jnp.exp(-m_t))
        hn = h * lax.rsqrt(jnp.mean(h * h, axis=1, keepdims=True) + EPS) * gh_ref[hd]
        o_ref[0, :, hd * dh:(hd + 1) * dh] = (hn * _sigmoid(og)).astype(o_ref.dtype)

        b_last = b_col[ln - 1:ln]
        m_new = m_t[ln - 1:ln]
        w_state = jnp.exp((b_last - b_col) + ig_col - m_new)
        decay = jnp.exp(b_last + m_prev - m_new)
        ks = k.astype(F32) * w_state
        c_scr[hd] = decay * c_prev + _dot_tn(ks.astype(BF16), v)
        n_scr[hd] = decay * n_prev + jnp.sum(ks, axis=0, keepdims=True)
        m_scr[hd] = jnp.broadcast_to(m_new, (1, LANES))

    @pl.when(ci == pl.num_programs(1) - 1)
    def _():
        c_out_ref[0] = c_scr[...]
        n_out_ref[0] = n_scr[...]
        m_out_ref[0] = m_scr[...]


def _mlstm(qk, vo, gc, gr, bias_c, bias_r, gh, c0, n0, m0):
    bn, t, c2 = qk.shape
    nh = N_HEADS_B
    dh = c2 // (2 * nh)
    ln = min(t, MLSTM_CHUNK)
    rows = gr.shape[1]
    state = lambda shape: pl.BlockSpec((1,) + shape, lambda b, i: (b, 0, 0, 0))
    return pl.pallas_call(
        _mlstm_kernel,
        out_shape=(jax.ShapeDtypeStruct((bn, t, nh * dh), BF16),
                   jax.ShapeDtypeStruct((bn, nh, dh, dh), F32),
                   jax.ShapeDtypeStruct((bn, nh, 1, dh), F32),
                   jax.ShapeDtypeStruct((bn, nh, 1, LANES), F32)),
        grid=(bn, t // ln),
        in_specs=[pl.BlockSpec((1, ln, c2), lambda b, i: (b, i, 0)),
                  pl.BlockSpec((1, ln, c2), lambda b, i: (b, i, 0)),
                  pl.BlockSpec((1, ln, LANES), lambda b, i: (b, i, 0)),
                  pl.BlockSpec((1, rows, ln), lambda b, i: (b, 0, i)),
                  pl.BlockSpec((1, LANES), lambda b, i: (0, 0)),
                  pl.BlockSpec((rows, 1), lambda b, i: (0, 0)),
                  pl.BlockSpec((nh, 1, dh), lambda b, i: (0, 0, 0)),
                  state((nh, dh, dh)), state((nh, 1, dh)), state((nh, 1, LANES))],
        out_specs=(pl.BlockSpec((1, ln, nh * dh), lambda b, i: (b, i, 0)),
                   state((nh, dh, dh)), state((nh, 1, dh)), state((nh, 1, LANES))),
        scratch_shapes=[pltpu.VMEM((nh, dh, dh), F32), pltpu.VMEM((nh, 1, dh), F32),
                        pltpu.VMEM((nh, 1, LANES), F32)],
        compiler_params=_params(("parallel", "arbitrary")),
        name="mlstm",
    )(qk, vo, gc, gr, bias_c, bias_r, gh, c0, n0, m0)


def _out_proj_kernel(x_ref, oa_ref, ob_ref, gt_ref, w_ref, sh_ref, sc_ref, g2_ref, o_ref, h_ref):
    bb, tt, d = x_ref.shape
    da, db = oa_ref.shape[2], ob_ref.shape[2]
    y = (_dot(oa_ref[...].reshape(bb * tt, da), w_ref[0:da])
         + _dot(ob_ref[...].reshape(bb * tt, db), w_ref[da:da + db]))
    x1 = x_ref[...] + gt_ref[...] * y.reshape(bb, tt, d)
    o_ref[...] = x1
    h_ref[...] = _modulated_norm(x1, g2_ref[...], sc_ref[...], sh_ref[...]).astype(BF16)


def _out_proj(x, oa, ob, mod, w, g2):
    bn, t, d = x.shape
    bb, tt = _row_blocks(bn, t, ROW_TILE)
    rows = lambda width: pl.BlockSpec((bb, tt, width), lambda b, i: (b, i, 0))
    return pl.pallas_call(
        _out_proj_kernel,
        out_shape=(jax.ShapeDtypeStruct((bn, t, d), F32), jax.ShapeDtypeStruct((bn, t, d), BF16)),
        grid=(bn // bb, t // tt),
        in_specs=[rows(d), rows(oa.shape[2]), rows(ob.shape[2]),
                  pl.BlockSpec((bb, 1, d), lambda b, i: (b, 0, 2)),
                  _resident(w.shape, lambda b, i: (0, 0)),
                  pl.BlockSpec((bb, 1, d), lambda b, i: (b, 0, 3)),
                  pl.BlockSpec((bb, 1, d), lambda b, i: (b, 0, 4)),
                  pl.BlockSpec((1, d), lambda b, i: (0, 0))],
        out_specs=(rows(d), rows(d)),
        compiler_params=_params(("parallel", "parallel")),
        name="out_proj",
    )(x, oa, ob, mod, w, mod, mod, g2)


def _ffn_kernel(x_ref, h_ref, gt_ref, w1_ref, w2_ref, o_ref, acc_scr):
    f = pl.program_id(2)
    bb, tt, d = x_ref.shape

    a = jnp.maximum(_dot(h_ref[...].reshape(bb * tt, d), w1_ref[...]), 0.0)
    acc = jnp.where(f == 0, 0.0, acc_scr[...]) + _dot((a * a).astype(BF16), w2_ref[...])
    acc_scr[...] = acc
    o_ref[...] = x_ref[...] + gt_ref[...] * acc.reshape(bb, tt, d)


def _ffn(x, h, mod, w1, w2):
    bn, t, d = x.shape
    dff = w1.shape[1]
    bb, tt = _row_blocks(bn, t, ROW_TILE)
    rows = pl.BlockSpec((bb, tt, d), lambda b, i, f: (b, i, 0))
    chunk = lambda c: pl.BlockSpec((bb, 1, d), lambda b, i, f: (b, 0, c))
    return pl.pallas_call(
        _ffn_kernel,
        out_shape=jax.ShapeDtypeStruct((bn, t, d), F32),
        grid=(bn // bb, t // tt, dff // FF_TILE),
        in_specs=[rows, rows, chunk(5),
                  pl.BlockSpec((d, FF_TILE), lambda b, i, f: (0, f)),
                  pl.BlockSpec((FF_TILE, d), lambda b, i, f: (f, 0))],
        out_specs=rows,
        scratch_shapes=[pltpu.VMEM((bb * tt, d), F32)],
        compiler_params=_params(("parallel", "parallel", "arbitrary")),
        name="ffn",
    )(x, h, mod, w1, w2)


def _layer(x, mod, past, wts):
    bn, t, d = x.shape
    db = d - d // 2
    nh, dh = N_HEADS_B, db // N_HEADS_B
    if past is None:
        k_past = v_past = None
        conv_prev = jnp.zeros((bn, CONV_W - 1, 2 * db), F32)
        c0 = jnp.zeros((bn, nh, dh, dh), F32)
        n0 = jnp.zeros((bn, nh, dh), F32)
        m0 = jnp.zeros((bn, nh), F32)
    else:
        k_past, v_past, c0, n0, m0, conv_prev = (a.astype(F32) for a in past)

    h, q, k, v, k_bf, v_bf = _proj_attn(x, mod, wts["g_norm1"], wts["w_in"], wts["g_q"], wts["g_k"])
    prev8 = jnp.pad(conv_prev, ((0, 0), (SUBLANES - (CONV_W - 1), 0), (0, 0)))
    qk, vo, gc, gr, conv_new = _proj_mlstm(h, wts["w_in"], wts["w_gate"], wts["w_gate_t"], prev8, wts["w_conv"],
                                           wts["b_conv"])
    o_a = _attn(q, k_bf, v_bf, k_past, v_past)
    o_b, c_new, n_new, m_new = _mlstm(
        qk, vo, gc, gr, wts["gate_bias_c"], wts["gate_bias_r"], wts["g_h"], c0, n0.reshape(bn, nh, 1, dh),
        jnp.broadcast_to(m0[:, :, None, None], (bn, nh, 1, LANES)))

    x, h2 = _out_proj(x, o_a, o_b, mod, wts["w_out"], wts["g_norm2"])
    x = _ffn(x, h2, mod, wts["w_ff1"], wts["w_ff2"])
    return x, (k, v, c_new, n_new.reshape(bn, nh, dh), m_new[:, :, 0, 0], conv_new)


def _layer_weights(l, w_in, g_norm1, g_q, g_k, w_conv, b_conv, b_i, b_f, g_h, w_out, g_norm2, w_ff1, w_ff2):
    d = w_in.shape[1]
    nh = N_HEADS_B
    w_gate = w_in[l][:, w_in.shape[2] - 2 * nh:]
    bias = jnp.concatenate([b_i[l], b_f[l]])
    rows = 2 * SUBLANES
    return {
        "w_in": w_in[l].astype(BF16),
        "w_gate": jnp.pad(w_gate, ((0, 0), (0, LANES - 2 * nh))).astype(BF16),
        "w_gate_t": jnp.pad(w_gate.T, ((0, rows - 2 * nh), (0, 0))).astype(BF16),
        "gate_bias_c": jnp.pad(bias, (0, LANES - 2 * nh)).reshape(1, LANES),
        "gate_bias_r": jnp.pad(bias, (0, rows - 2 * nh)).reshape(rows, 1),
        "g_norm1": g_norm1[l].reshape(1, d), "g_norm2": g_norm2[l].reshape(1, d),
        "g_q": g_q[l].reshape(1, -1), "g_k": g_k[l].reshape(1, -1),
        "w_conv": w_conv[l], "b_conv": b_conv[l].reshape(1, -1),
        "g_h": g_h[l].reshape(nh, 1, -1),
        "w_out": w_out[l].astype(BF16), "w_ff1": w_ff1[l].astype(BF16), "w_ff2": w_ff2[l].astype(BF16),
    }


def kernel(x_prompt, x_sample, c_prompt, c_sample, cache_k, cache_v, state_C, state_n, state_m, state_conv,
           w_ada, b_ada, g_norm1, w_in, g_q, g_k, w_conv, b_conv, b_i, b_f, g_h, w_out, g_norm2, w_ff1, w_ff2):
    depth = w_ada.shape[0]
    bp, bs = c_prompt.shape[0], c_sample.shape[0]
    c_rows = -(-(bp + bs) // (2 * SUBLANES)) * (2 * SUBLANES)
    c_all = jnp.pad(jnp.concatenate([c_prompt, c_sample], axis=0), ((0, c_rows - bp - bs), (0, 0)))
    y_prompt, y_sample = x_prompt, x_sample
    new_p, new_s = [], []
    for l in range(depth):
        wts = _layer_weights(l, w_in, g_norm1, g_q, g_k, w_conv, b_conv, b_i, b_f, g_h, w_out, g_norm2,
                             w_ff1, w_ff2)
        mod = _ada(c_all, w_ada[l], b_ada[l].reshape(1, -1))
        mod_p = mod[:bp].reshape(bp, 1, -1)
        mod_s = mod[bp:bp + bs].reshape(bs, 1, -1)
        y_prompt, sp = _layer(y_prompt, mod_p, None, wts)
        past = (cache_k[l], cache_v[l], state_C[l], state_n[l], state_m[l], state_conv[l])
        y_sample, ss = _layer(y_sample, mod_s, past, wts)
        new_p.append(sp)
        new_s.append(ss)
    stack = lambda states, i: jnp.stack([s[i] for s in states], axis=0)
    return ((y_prompt, y_sample) + tuple(stack(new_p, i) for i in range(6))
            + tuple(stack(new_s, i) for i in range(6)))
```

```python
import functools

import jax
import jax.numpy as jnp
from jax import lax
from jax.experimental import pallas as pl
from jax.experimental.pallas import tpu as pltpu

F32, BF16 = jnp.float32, jnp.bfloat16
EPS = 1e-6
HEAD_DIM_A = 128
N_HEADS_B = 4
CONV_W = 4
LANES = 128
SUBLANES = 8
F32_EXP2_UNDERFLOW = 151.0
LOG2_E = 1.4426950408889634
NO_PAST = 1e30
V7X_VMEM_BYTES = 64 * 2 ** 20
VMEM_LIMIT = V7X_VMEM_BYTES * 7 // 8

ROW_TILE = 512
ATTN_BLOCK = 256
SAMPLE_KEY_BLOCK = 256
MLSTM_CHUNK = 256
FF_TILE = 1024
ADA_TILE = 1024
ATTN_HEADS_PER_STEP = 4
SUFFIX_SUM_TERMS = 1
PROJ_COLS = 256


def _params(semantics, vmem_limit=VMEM_LIMIT):
    return pltpu.CompilerParams(dimension_semantics=semantics, vmem_limit_bytes=vmem_limit)


def _resident(shape, index_map):
    return pl.BlockSpec(shape, index_map, pipeline_mode=pl.Buffered(1))


def _sigmoid(x):
    return 1.0 / (1.0 + jnp.exp(-x))


def _log_sigmoid(x):
    return jnp.minimum(x, 0.0) - jnp.log1p(jnp.exp(-jnp.abs(x)))


def _dot(a, b):
    return jnp.dot(a, b, preferred_element_type=F32)


def _dot_nt(a, b):
    return lax.dot_general(a, b, (((1,), (1,)), ((), ())), preferred_element_type=F32)


def _dot_tn(a, b):
    return lax.dot_general(a, b, (((0,), (0,)), ((), ())), preferred_element_type=F32)


def _split_bf16(x, parts):
    out = []
    for _ in range(parts - 1):
        p = x.astype(BF16)
        out.append(p)
        x = x - p.astype(F32)
    out.append(x.astype(BF16))
    return out


def _modulated_norm(x, g, sc, sh):
    y = x * lax.rsqrt(jnp.mean(x * x, axis=-1, keepdims=True) + EPS)
    return (y * g) * (1.0 + sc) + sh


def _row_blocks(bn, t, rows):
    if t >= rows:
        return 1, rows
    return min(bn, rows // t), t


def _ada_kernel(c_ref, w_ref, b_ref, o_ref):
    c = c_ref[...]
    s = (c * _sigmoid(c)).astype(BF16)
    o_ref[...] = _dot(s, w_ref[...].astype(BF16)) + b_ref[...]


def _ada(c, w, b):
    m, d = c.shape
    n = w.shape[1]
    return pl.pallas_call(
        _ada_kernel,
        out_shape=jax.ShapeDtypeStruct((m, n), F32),
        grid=(n // ADA_TILE,),
        in_specs=[pl.BlockSpec((m, d), lambda j: (0, 0)),
                  pl.BlockSpec((d, ADA_TILE), lambda j: (0, j)),
                  pl.BlockSpec((1, ADA_TILE), lambda j: (0, j))],
        out_specs=pl.BlockSpec((m, ADA_TILE), lambda j: (0, j)),
        compiler_params=_params(("parallel",)),
        name="ada",
    )(c, w, b)


def _proj_attn_kernel(x_ref, sh_ref, sc_ref, g1_ref, w_ref, gq_ref, gk_ref, h_ref, q_ref, k_ref, v_ref, kb_ref,
                      vb_ref):
    bb, tt, d = x_ref.shape
    nh = q_ref.shape[1]
    da = nh * HEAD_DIM_A
    h3 = _modulated_norm(x_ref[...], g1_ref[...], sc_ref[...], sh_ref[...]).astype(BF16)
    h_ref[...] = h3
    hb = h3.reshape(bb * tt, d)
    outs = (((q_ref,), gq_ref), ((k_ref, kb_ref), gk_ref), ((v_ref, vb_ref), None))
    for part, (out_refs, g_ref) in enumerate(outs):
        u = _dot(hb, w_ref[:, part * da:(part + 1) * da])
        for hd in range(nh):
            uh = u[:, hd * HEAD_DIM_A:(hd + 1) * HEAD_DIM_A]
            if g_ref is not None:
                uh = uh * lax.rsqrt(jnp.mean(uh * uh, axis=-1, keepdims=True) + EPS) * g_ref[...]
            for out_ref in out_refs:
                out_ref[:, hd] = uh.reshape(bb, tt, HEAD_DIM_A).astype(out_ref.dtype)


def _proj_attn(x, mod, g1, w, gq, gk):
    bn, t, d = x.shape
    nh = d // 2 // HEAD_DIM_A
    bb, tt = _row_blocks(bn, t, ROW_TILE)
    head_spec = pl.BlockSpec((bb, nh, tt, HEAD_DIM_A), lambda b, i: (b, 0, i, 0))
    return pl.pallas_call(
        _proj_attn_kernel,
        out_shape=(jax.ShapeDtypeStruct((bn, t, d), BF16),) + tuple(
            jax.ShapeDtypeStruct((bn, nh, t, HEAD_DIM_A), dt) for dt in (BF16, F32, F32, BF16, BF16)),
        grid=(bn // bb, t // tt),
        in_specs=[pl.BlockSpec((bb, tt, d), lambda b, i: (b, i, 0)),
                  pl.BlockSpec((bb, 1, d), lambda b, i: (b, 0, 0)),
                  pl.BlockSpec((bb, 1, d), lambda b, i: (b, 0, 1)),
                  pl.BlockSpec((1, d), lambda b, i: (0, 0)),
                  _resident((d, 3 * nh * HEAD_DIM_A), lambda b, i: (0, 0)),
                  pl.BlockSpec((1, HEAD_DIM_A), lambda b, i: (0, 0)),
                  pl.BlockSpec((1, HEAD_DIM_A), lambda b, i: (0, 0))],
        out_specs=(pl.BlockSpec((bb, tt, d), lambda b, i: (b, i, 0)),) + (head_spec,) * 5,
        compiler_params=_params(("parallel", "parallel")),
        name="proj_attn",
    )(x, mod, mod, g1, w, gq, gk)


def _proj_mlstm_kernel(h_ref, wq_ref, wk_ref, wv_ref, wo_ref, wg_ref, wgt_ref, raw_ref, vo_ref, gc_ref, gr_ref):
    bb, tt, d = h_ref.shape
    c = raw_ref.shape[2]
    hb = h_ref[...].reshape(bb * tt, d)
    db = c // 2
    step = min(4 * PROJ_COLS, db)

    def w_cols(lo):
        return (wq_ref, wk_ref, wv_ref, wo_ref)[lo // db][:, lo % db:lo % db + step]

    for lo in range(0, c, step):
        raw_ref[:, :, lo:lo + step] = _dot(hb, w_cols(lo)).reshape(bb, tt, step)
        vo_ref[:, :, lo:lo + step] = _dot(hb, w_cols(c + lo)).reshape(bb, tt, step).astype(vo_ref.dtype)
    gc_ref[...] = _dot(hb, wg_ref[...]).reshape(bb, tt, LANES)
    for b in range(bb):
        gr_ref[b] = _dot_nt(wgt_ref[...], hb[b * tt:(b + 1) * tt])


def _proj_mlstm(h, w, wg, wgt):
    bn, t, d = h.shape
    db = d - d // 2
    c = 2 * db
    first = (3 * (d // 2)) // db
    bb, tt = _row_blocks(bn, t, ROW_TILE)
    return pl.pallas_call(
        _proj_mlstm_kernel,
        out_shape=(jax.ShapeDtypeStruct((bn, t, c), F32),
                   jax.ShapeDtypeStruct((bn, t, c), BF16),
                   jax.ShapeDtypeStruct((bn, t, LANES), F32),
                   jax.ShapeDtypeStruct((bn, wgt.shape[0], t), F32)),
        grid=(bn // bb, t // tt),
        in_specs=[pl.BlockSpec((bb, tt, d), lambda b, i: (b, i, 0)),
                  *[_resident((d, db), lambda b, i, j=j: (0, first + j)) for j in range(4)],
                  _resident(wg.shape, lambda b, i: (0, 0)),
                  _resident(wgt.shape, lambda b, i: (0, 0))],
        out_specs=(pl.BlockSpec((bb, tt, c), lambda b, i: (b, i, 0)),
                   pl.BlockSpec((bb, tt, c), lambda b, i: (b, i, 0)),
                   pl.BlockSpec((bb, tt, LANES), lambda b, i: (b, i, 0)),
                   pl.BlockSpec((bb, wgt.shape[0], tt), lambda b, i: (b, 0, i))),
        compiler_params=_params(("parallel", "parallel")),
        name="proj_mlstm",
    )(h, w, w, w, w, wg, wgt)


def _strict_upper_ones(n):
    r = lax.broadcasted_iota(jnp.int32, (n, n), 0)
    c = lax.broadcasted_iota(jnp.int32, (n, n), 1)
    return (r > c).astype(BF16)


def _sb_block(q, kb, vb, tri, carry, mask):
    z = _dot_nt(q, kb) * (HEAD_DIM_A ** -0.5 * LOG2_E)
    log_beta = jnp.minimum(z, 0.0) - jnp.log2(1.0 + jnp.exp2(-jnp.abs(z)))
    log_1m = log_beta - z
    if mask is not None:
        log_1m = jnp.where(mask, log_1m, 0.0)
    tail = sum(_dot(p, tri) for p in _split_bf16(log_1m, SUFFIX_SUM_TERMS))
    total = tail[:, :1] + log_1m[:, :1]
    if carry is not None:
        tail = tail + carry
    a = jnp.exp2(log_beta + tail)
    if mask is not None:
        a = jnp.where(mask, a, 0.0)
    return _dot(a.astype(BF16), vb), total


def _attn_kernel(q_ref, kd_ref, vd_ref, kp_ref, vp_ref, o_ref, *maybe_carry_ref, tk, past_is_prefix):
    hps, tq, dh = q_ref.shape[1], q_ref.shape[2], q_ref.shape[3]
    row = lax.broadcasted_iota(jnp.int32, (tq, tq), 0)
    col = lax.broadcasted_iota(jnp.int32, (tq, tq), 1)
    n_past = pl.program_id(2) * (tq // tk) if past_is_prefix else kp_ref.shape[2] // tk
    tri = _strict_upper_ones(tk)
    tri_diag = tri if tq == tk else _strict_upper_ones(tq)

    def past_block(h, jj, carry):
        start = pl.multiple_of(jnp.maximum(n_past - 1 - jj, 0) * tk, tk)
        kb = kp_ref[0, h, pl.ds(start, tk), :].astype(BF16)
        vb = vp_ref[0, h, pl.ds(start, tk), :].astype(BF16)
        return _sb_block(q_ref[0, h], kb, vb, tri, carry, None)

    state = []
    for h in range(hps):
        acc, s = _sb_block(q_ref[0, h], kd_ref[0, h].astype(BF16), vd_ref[0, h].astype(BF16), tri_diag, None,
                           col < row)
        pv, ds = past_block(h, 0, jnp.where(n_past > 0, s, -NO_PAST))
        state.append((acc + pv, s + ds))

    worst = None
    for h, (acc, s) in enumerate(state):
        def more(c):
            jj, _, s = c
            return jnp.logical_and(jj < n_past, jnp.max(s) > -F32_EXP2_UNDERFLOW)

        def body(c, h=h):
            jj, acc, s = c
            pv, ds = past_block(h, jj, s)
            return jj + 1, acc + pv, s + ds

        _, acc, s = lax.while_loop(more, body, (jnp.int32(1), acc, s))
        o_ref[0, :, h * dh:(h + 1) * dh] = acc.astype(o_ref.dtype)
        if maybe_carry_ref:
            top = jnp.max(s, axis=0, keepdims=True)
            worst = top if worst is None else jnp.maximum(worst, top)
    if maybe_carry_ref:
        carry_ref, = maybe_carry_ref
        carry_ref[0, 0, 0] = jnp.broadcast_to(worst, carry_ref.shape[3:])


def _attn_call(q, k, v, k_past, v_past, prefix, window):
    bn, nh, t, dh = q.shape
    if prefix:
        tq = tk = min(t, ATTN_BLOCK)
    else:
        tq, tk = t, min(k_past.shape[2], SAMPLE_KEY_BLOCK)
    p = k_past.shape[2] if window is None else window
    last = k_past.shape[2] // p - 1
    hps = min(nh, ATTN_HEADS_PER_STEP)
    blk = pl.BlockSpec((1, hps, tq, dh), lambda b, h, i: (b, h, i, 0))
    past = pl.BlockSpec((1, hps, p, dh), lambda b, h, i: (b, h, last, 0))
    out_shape = [jax.ShapeDtypeStruct((bn, t, nh * dh), BF16)]
    out_specs = [pl.BlockSpec((1, tq, hps * dh), lambda b, h, i: (b, i, h))]
    if window is not None:
        out_shape.append(jax.ShapeDtypeStruct((bn, nh // hps, t // tq, SUBLANES, LANES), F32))
        out_specs.append(pl.BlockSpec((1, 1, 1, SUBLANES, LANES), lambda b, h, i: (b, h, i, 0, 0)))
    return pl.pallas_call(
        functools.partial(_attn_kernel, tk=tk, past_is_prefix=prefix),
        out_shape=tuple(out_shape),
        grid=(bn, nh // hps, t // tq),
        in_specs=[blk, blk, blk, past, past],
        out_specs=tuple(out_specs),
        compiler_params=_params(("parallel", "parallel", "parallel")),
        name="attn",
    )(q, k, v, k_past, v_past)


def _attn(q, k, v, k_past, v_past):
    if k_past is None:
        return _attn_call(q, k, v, k, v, True, None)[0]
    near, carry = _attn_call(q, k, v, k_past, v_past, False, min(k_past.shape[2], SAMPLE_KEY_BLOCK))
    return lax.cond(jnp.max(carry) > -F32_EXP2_UNDERFLOW,
                    lambda: _attn_call(q, k, v, k_past, v_past, False, None)[0],
                    lambda: near)


def _mlstm_kernel(raw_ref, vo_ref, gc_ref, gr_ref, bc_ref, br_ref, gh_ref, c0_ref, n0_ref, m0_ref, prev_ref,
                  wc_ref, cb_ref, o_ref, c_out_ref, n_out_ref, m_out_ref, new_ref, c_scr, n_scr, m_scr, halo_scr):
    ci = pl.program_id(1)
    ln = raw_ref.shape[1]
    nh = N_HEADS_B
    dh = raw_ref.shape[2] // (2 * nh)

    @pl.when(ci == 0)
    def _():
        c_scr[...] = c0_ref[0]
        n_scr[...] = n0_ref[0]
        m_scr[...] = m0_ref[0]
        halo_scr[...] = prev_ref[0]

    def conv(cols, scale):
        u = raw_ref[0, :, cols]
        ext = jnp.concatenate([halo_scr[:, cols], u], axis=0)
        y = pltpu.roll(ext, CONV_W - 1, axis=0)[SUBLANES:] * wc_ref[0:1, cols]
        for j in range(1, CONV_W - 1):
            y = y + pltpu.roll(ext, CONV_W - 1 - j, axis=0)[SUBLANES:] * wc_ref[j:j + 1, cols]
        y = y + u * wc_ref[CONV_W - 1:CONV_W, cols] + cb_ref[:, cols]
        halo_scr[:, cols] = ext[ln:ln + SUBLANES]
        return (y * _sigmoid(y) * scale).astype(BF16)

    gc = gc_ref[0] + bc_ref[...]
    gr = gr_ref[0] + br_ref[...]
    lfc = _log_sigmoid(gc)
    lfr = _log_sigmoid(gr)
    row = lax.broadcasted_iota(jnp.int32, (ln, ln), 0)
    col = lax.broadcasted_iota(jnp.int32, (ln, ln), 1)
    causal = col <= row
    ones_causal = causal.astype(BF16)
    ones_causal_t = (row <= col).astype(BF16)
    b_rows = sum(_dot(p, ones_causal_t) for p in _split_bf16(lfr, 3))

    for hd in range(nh):
        q = conv(slice(hd * dh, (hd + 1) * dh), 1.0)
        k = conv(slice((nh + hd) * dh, (nh + hd + 1) * dh), dh ** -0.5)
        v = vo_ref[0, :, hd * dh:(hd + 1) * dh]
        og = vo_ref[0, :, (nh + hd) * dh:(nh + hd + 1) * dh].astype(F32)
        ig_col = gc[:, hd:hd + 1]
        ig_row = gr[hd:hd + 1, :]
        b_row = b_rows[nh + hd:nh + hd + 1, :]
        lf_b = jnp.broadcast_to(lfc[:, nh + hd:nh + hd + 1], (ln, ln))
        bt = sum(_dot(ones_causal, p) for p in _split_bf16(lf_b, 3))
        b_col = bt[:, :1]
        m_prev = m_scr[hd][:, :1]
        c_prev = c_scr[hd]
        n_prev = n_scr[hd]

        dmat = jnp.where(causal, (bt - b_row) + ig_row, -jnp.inf)
        inter = b_col + m_prev
        m_t = jnp.maximum(inter, jnp.max(dmat, axis=1, keepdims=True))
        w_inter = jnp.exp(inter - m_t)
        s = jnp.exp(dmat - m_t) * _dot_nt(q, k)
        num = w_inter * _dot(q, c_prev.astype(BF16)) + _dot(s.astype(BF16), v)
        den = (w_inter * jnp.sum(q.astype(F32) * n_prev, axis=1, keepdims=True)
               + jnp.sum(s, axis=1, keepdims=True))
        h = num / jnp.maximum(jnp.abs(den), jnp.exp(-m_t))
        hn = h * lax.rsqrt(jnp.mean(h * h, axis=1, keepdims=True) + EPS) * gh_ref[hd]
        o_ref[0, :, hd * dh:(hd + 1) * dh] = (hn * _sigmoid(og)).astype(o_ref.dtype)

        b_last = b_col[ln - 1:ln]
        m_new = m_t[ln - 1:ln]
        w_state = jnp.exp((b_last - b_col) + ig_col - m_new)
        decay = jnp.exp(b_last + m_prev - m_new)
        ks = k.astype(F32) * w_state
        c_scr[hd] = decay * c_prev + _dot_tn(ks.astype(BF16), v)
        n_scr[hd] = decay * n_prev + jnp.sum(ks, axis=0, keepdims=True)
        m_scr[hd] = jnp.broadcast_to(m_new, (1, LANES))

    @pl.when(ci == pl.num_programs(1) - 1)
    def _():
        c_out_ref[0] = c_scr[...]
        n_out_ref[0] = n_scr[...]
        m_out_ref[0] = m_scr[...]
        new_ref[0] = halo_scr[SUBLANES - (CONV_W - 1):SUBLANES]


def _mlstm(raw, vo, gc, gr, bias_c, bias_r, gh, c0, n0, m0, prev8, wc, cb):
    bn, t, c2 = raw.shape
    nh = N_HEADS_B
    dh = c2 // (2 * nh)
    ln = min(t, MLSTM_CHUNK)
    rows = gr.shape[1]
    state = lambda shape: pl.BlockSpec((1,) + shape, lambda b, i: (b, 0, 0, 0))
    return pl.pallas_call(
        _mlstm_kernel,
        out_shape=(jax.ShapeDtypeStruct((bn, t, nh * dh), BF16),
                   jax.ShapeDtypeStruct((bn, nh, dh, dh), F32),
                   jax.ShapeDtypeStruct((bn, nh, 1, dh), F32),
                   jax.ShapeDtypeStruct((bn, nh, 1, LANES), F32),
                   jax.ShapeDtypeStruct((bn, CONV_W - 1, c2), F32)),
        grid=(bn, t // ln),
        in_specs=[pl.BlockSpec((1, ln, c2), lambda b, i: (b, i, 0)),
                  pl.BlockSpec((1, ln, c2), lambda b, i: (b, i, 0)),
                  pl.BlockSpec((1, ln, LANES), lambda b, i: (b, i, 0)),
                  pl.BlockSpec((1, rows, ln), lambda b, i: (b, 0, i)),
                  pl.BlockSpec((1, LANES), lambda b, i: (0, 0)),
                  pl.BlockSpec((rows, 1), lambda b, i: (0, 0)),
                  pl.BlockSpec((nh, 1, dh), lambda b, i: (0, 0, 0)),
                  state((nh, dh, dh)), state((nh, 1, dh)), state((nh, 1, LANES)),
                  pl.BlockSpec((1, SUBLANES, c2), lambda b, i: (b, 0, 0)),
                  pl.BlockSpec((CONV_W, c2), lambda b, i: (0, 0)),
                  pl.BlockSpec((1, c2), lambda b, i: (0, 0))],
        out_specs=(pl.BlockSpec((1, ln, nh * dh), lambda b, i: (b, i, 0)),
                   state((nh, dh, dh)), state((nh, 1, dh)), state((nh, 1, LANES)),
                   pl.BlockSpec((1, CONV_W - 1, c2), lambda b, i: (b, 0, 0))),
        scratch_shapes=[pltpu.VMEM((nh, dh, dh), F32), pltpu.VMEM((nh, 1, dh), F32),
                        pltpu.VMEM((nh, 1, LANES), F32), pltpu.VMEM((SUBLANES, c2), F32)],
        compiler_params=_params(("parallel", "arbitrary")),
        name="mlstm",
    )(raw, vo, gc, gr, bias_c, bias_r, gh, c0, n0, m0, prev8, wc, cb)


def _out_proj_kernel(x_ref, oa_ref, ob_ref, gt_ref, w_ref, sh_ref, sc_ref, g2_ref, o_ref, h_ref):
    bb, tt, d = x_ref.shape
    da, db = oa_ref.shape[2], ob_ref.shape[2]
    y = (_dot(oa_ref[...].reshape(bb * tt, da), w_ref[0:da])
         + _dot(ob_ref[...].reshape(bb * tt, db), w_ref[da:da + db]))
    x1 = x_ref[...] + gt_ref[...] * y.reshape(bb, tt, d)
    o_ref[...] = x1
    h_ref[...] = _modulated_norm(x1, g2_ref[...], sc_ref[...], sh_ref[...]).astype(BF16)


def _out_proj(x, oa, ob, mod, w, g2):
    bn, t, d = x.shape
    bb, tt = _row_blocks(bn, t, ROW_TILE)
    rows = lambda width: pl.BlockSpec((bb, tt, width), lambda b, i: (b, i, 0))
    return pl.pallas_call(
        _out_proj_kernel,
        out_shape=(jax.ShapeDtypeStruct((bn, t, d), F32), jax.ShapeDtypeStruct((bn, t, d), BF16)),
        grid=(bn // bb, t // tt),
        in_specs=[rows(d), rows(oa.shape[2]), rows(ob.shape[2]),
                  pl.BlockSpec((bb, 1, d), lambda b, i: (b, 0, 2)),
                  _resident(w.shape, lambda b, i: (0, 0)),
                  pl.BlockSpec((bb, 1, d), lambda b, i: (b, 0, 3)),
                  pl.BlockSpec((bb, 1, d), lambda b, i: (b, 0, 4)),
                  pl.BlockSpec((1, d), lambda b, i: (0, 0))],
        out_specs=(rows(d), rows(d)),
        compiler_params=_params(("parallel", "parallel")),
        name="out_proj",
    )(x, oa, ob, mod, w, mod, mod, g2)


def _ffn_kernel(x_ref, h_ref, gt_ref, w1_ref, w2_ref, o_ref, acc_scr):
    f = pl.program_id(2)
    bb, tt, d = x_ref.shape

    a = jnp.maximum(_dot(h_ref[...].reshape(bb * tt, d), w1_ref[...]), 0.0)
    acc = jnp.where(f == 0, 0.0, acc_scr[...]) + _dot((a * a).astype(BF16), w2_ref[...])
    acc_scr[...] = acc
    o_ref[...] = x_ref[...] + gt_ref[...] * acc.reshape(bb, tt, d)


def _ffn(x, h, mod, w1, w2):
    bn, t, d = x.shape
    dff = w1.shape[1]
    bb, tt = _row_blocks(bn, t, ROW_TILE)
    rows = pl.BlockSpec((bb, tt, d), lambda b, i, f: (b, i, 0))
    chunk = lambda c: pl.BlockSpec((bb, 1, d), lambda b, i, f: (b, 0, c))
    return pl.pallas_call(
        _ffn_kernel,
        out_shape=jax.ShapeDtypeStruct((bn, t, d), F32),
        grid=(bn // bb, t // tt, dff // FF_TILE),
        in_specs=[rows, rows, chunk(5),
                  pl.BlockSpec((d, FF_TILE), lambda b, i, f: (0, f)),
                  pl.BlockSpec((FF_TILE, d), lambda b, i, f: (f, 0))],
        out_specs=rows,
        scratch_shapes=[pltpu.VMEM((bb * tt, d), F32)],
        compiler_params=_params(("parallel", "parallel", "arbitrary")),
        name="ffn",
    )(x, h, mod, w1, w2)


def _layer(x, mod, past, wts):
    bn, t, d = x.shape
    db = d - d // 2
    nh, dh = N_HEADS_B, db // N_HEADS_B
    if past is None:
        k_past = v_past = None
        conv_prev = jnp.zeros((bn, CONV_W - 1, 2 * db), F32)
        c0 = jnp.zeros((bn, nh, dh, dh), F32)
        n0 = jnp.zeros((bn, nh, dh), F32)
        m0 = jnp.zeros((bn, nh), F32)
    else:
        k_past, v_past, c0, n0, m0, conv_prev = (a.astype(F32) for a in past)

    h, q, k, v, k_bf, v_bf = _proj_attn(x, mod, wts["g_norm1"], wts["w_in"], wts["g_q"], wts["g_k"])
    prev8 = jnp.pad(conv_prev, ((0, 0), (SUBLANES - (CONV_W - 1), 0), (0, 0)))
    raw, vo, gc, gr = _proj_mlstm(h, wts["w_in"], wts["w_gate"], wts["w_gate_t"])
    o_a = _attn(q, k_bf, v_bf, k_past, v_past)
    o_b, c_new, n_new, m_new, conv_new = _mlstm(
        raw, vo, gc, gr, wts["gate_bias_c"], wts["gate_bias_r"], wts["g_h"], c0, n0.reshape(bn, nh, 1, dh),
        jnp.broadcast_to(m0[:, :, None, None], (bn, nh, 1, LANES)), prev8, wts["w_conv"], wts["b_conv"])

    x, h2 = _out_proj(x, o_a, o_b, mod, wts["w_out"], wts["g_norm2"])
    x = _ffn(x, h2, mod, wts["w_ff1"], wts["w_ff2"])
    return x, (k, v, c_new, n_new.reshape(bn, nh, dh), m_new[:, :, 0, 0], conv_new)


def _layer_weights(l, w_in, g_norm1, g_q, g_k, w_conv, b_conv, b_i, b_f, g_h, w_out, g_norm2, w_ff1, w_ff2):
    d = w_in.shape[1]
    nh = N_HEADS_B
    w_gate = w_in[l][:, w_in.shape[2] - 2 * nh:]
    bias = jnp.concatenate([b_i[l], b_f[l]])
    rows = 2 * SUBLANES
    return {
        "w_in": w_in[l].astype(BF16),
        "w_gate": jnp.pad(w_gate, ((0, 0), (0, LANES - 2 * nh))).astype(BF16),
        "w_gate_t": jnp.pad(w_gate.T, ((0, rows - 2 * nh), (0, 0))).astype(BF16),
        "gate_bias_c": jnp.pad(bias, (0, LANES - 2 * nh)).reshape(1, LANES),
        "gate_bias_r": jnp.pad(bias, (0, rows - 2 * nh)).reshape(rows, 1),
        "g_norm1": g_norm1[l].reshape(1, d), "g_norm2": g_norm2[l].reshape(1, d),
        "g_q": g_q[l].reshape(1, -1), "g_k": g_k[l].reshape(1, -1),
        "w_conv": w_conv[l], "b_conv": b_conv[l].reshape(1, -1),
        "g_h": g_h[l].reshape(nh, 1, -1),
        "w_out": w_out[l].astype(BF16), "w_ff1": w_ff1[l].astype(BF16), "w_ff2": w_ff2[l].astype(BF16),
    }


def kernel(x_prompt, x_sample, c_prompt, c_sample, cache_k, cache_v, state_C, state_n, state_m, state_conv,
           w_ada, b_ada, g_norm1, w_in, g_q, g_k, w_conv, b_conv, b_i, b_f, g_h, w_out, g_norm2, w_ff1, w_ff2):
    depth = w_ada.shape[0]
    bp, bs = c_prompt.shape[0], c_sample.shape[0]
    c_rows = -(-(bp + bs) // (2 * SUBLANES)) * (2 * SUBLANES)
    c_all = jnp.pad(jnp.concatenate([c_prompt, c_sample], axis=0), ((0, c_rows - bp - bs), (0, 0)))
    y_prompt, y_sample = x_prompt, x_sample
    new_p, new_s = [], []
    for l in range(depth):
        wts = _layer_weights(l, w_in, g_norm1, g_q, g_k, w_conv, b_conv, b_i, b_f, g_h, w_out, g_norm2,
                             w_ff1, w_ff2)
        mod = _ada(c_all, w_ada[l], b_ada[l].reshape(1, -1))
        mod_p = mod[:bp].reshape(bp, 1, -1)
        mod_s = mod[bp:bp + bs].reshape(bs, 1, -1)
        y_prompt, sp = _layer(y_prompt, mod_p, None, wts)
        past = (cache_k[l], cache_v[l], state_C[l], state_n[l], state_m[l], state_conv[l])
        y_sample, ss = _layer(y_sample, mod_s, past, wts)
        new_p.append(sp)
        new_s.append(ss)
    stack = lambda states, i: jnp.stack([s[i] for s in states], axis=0)
    return ((y_prompt, y_sample) + tuple(stack(new_p, i) for i in range(6))
            + tuple(stack(new_s, i) for i in range(6)))
```

```python
import functools

import jax
import jax.numpy as jnp
from jax import lax
from jax.experimental import pallas as pl
from jax.experimental.pallas import tpu as pltpu

F32, BF16 = jnp.float32, jnp.bfloat16
EPS = 1e-6
HEAD_DIM_A = 128
N_HEADS_B = 4
CONV_W = 4
LANES = 128
SUBLANES = 8
F32_EXP2_UNDERFLOW = 151.0
LOG2_E = 1.4426950408889634
NO_PAST = 1e30
V7X_VMEM_BYTES = 64 * 2 ** 20
VMEM_LIMIT = V7X_VMEM_BYTES * 7 // 8

ROW_TILE = 512
ATTN_BLOCK = 256
SAMPLE_KEY_BLOCK = 256
MLSTM_CHUNK = 256
FF_TILE = 1024
ADA_TILE = 1024
ATTN_HEADS_PER_STEP = 4
SUFFIX_SUM_TERMS = 1
PROJ_COLS = 256


def _params(semantics, vmem_limit=VMEM_LIMIT):
    return pltpu.CompilerParams(dimension_semantics=semantics, vmem_limit_bytes=vmem_limit)


def _resident(shape, index_map):
    return pl.BlockSpec(shape, index_map, pipeline_mode=pl.Buffered(1))


def _sigmoid(x):
    return 0.5 * jnp.tanh(0.5 * x) + 0.5


def _log_sigmoid(x):
    return jnp.minimum(x, 0.0) - jnp.log1p(jnp.exp(-jnp.abs(x)))


def _dot(a, b):
    return jnp.dot(a, b, preferred_element_type=F32)


def _dot_nt(a, b):
    return lax.dot_general(a, b, (((1,), (1,)), ((), ())), preferred_element_type=F32)


def _dot_tn(a, b):
    return lax.dot_general(a, b, (((0,), (0,)), ((), ())), preferred_element_type=F32)


def _split_bf16(x, parts):
    out = []
    for _ in range(parts - 1):
        p = x.astype(BF16)
        out.append(p)
        x = x - p.astype(F32)
    out.append(x.astype(BF16))
    return out


def _modulated_norm(x, g, sc, sh):
    y = x * lax.rsqrt(jnp.mean(x * x, axis=-1, keepdims=True) + EPS)
    return (y * g) * (1.0 + sc) + sh


def _row_blocks(bn, t, rows):
    if t >= rows:
        return 1, rows
    return min(bn, rows // t), t


def _ada_kernel(c_ref, w_ref, b_ref, o_ref):
    c = c_ref[...]
    s = (c * _sigmoid(c)).astype(BF16)
    o_ref[...] = _dot(s, w_ref[...].astype(BF16)) + b_ref[...]


def _ada(c, w, b):
    m, d = c.shape
    n = w.shape[1]
    return pl.pallas_call(
        _ada_kernel,
        out_shape=jax.ShapeDtypeStruct((m, n), F32),
        grid=(n // ADA_TILE,),
        in_specs=[pl.BlockSpec((m, d), lambda j: (0, 0)),
                  pl.BlockSpec((d, ADA_TILE), lambda j: (0, j)),
                  pl.BlockSpec((1, ADA_TILE), lambda j: (0, j))],
        out_specs=pl.BlockSpec((m, ADA_TILE), lambda j: (0, j)),
        compiler_params=_params(("parallel",)),
        name="ada",
    )(c, w, b)


def _proj_attn_kernel(x_ref, sh_ref, sc_ref, g1_ref, w_ref, gq_ref, gk_ref, h_ref, q_ref, k_ref, v_ref, kb_ref,
                      vb_ref):
    bb, tt, d = x_ref.shape
    nh = q_ref.shape[1]
    da = nh * HEAD_DIM_A
    h3 = _modulated_norm(x_ref[...], g1_ref[...], sc_ref[...], sh_ref[...]).astype(BF16)
    h_ref[...] = h3
    hb = h3.reshape(bb * tt, d)
    outs = (((q_ref,), gq_ref), ((k_ref, kb_ref), gk_ref), ((v_ref, vb_ref), None))
    for part, (out_refs, g_ref) in enumerate(outs):
        u = _dot(hb, w_ref[:, part * da:(part + 1) * da])
        for hd in range(nh):
            uh = u[:, hd * HEAD_DIM_A:(hd + 1) * HEAD_DIM_A]
            if g_ref is not None:
                uh = uh * lax.rsqrt(jnp.mean(uh * uh, axis=-1, keepdims=True) + EPS) * g_ref[...]
            for out_ref in out_refs:
                out_ref[:, hd] = uh.reshape(bb, tt, HEAD_DIM_A).astype(out_ref.dtype)


def _proj_attn(x, mod, g1, w, gq, gk):
    bn, t, d = x.shape
    nh = d // 2 // HEAD_DIM_A
    bb, tt = _row_blocks(bn, t, ROW_TILE)
    head_spec = pl.BlockSpec((bb, nh, tt, HEAD_DIM_A), lambda b, i: (b, 0, i, 0))
    return pl.pallas_call(
        _proj_attn_kernel,
        out_shape=(jax.ShapeDtypeStruct((bn, t, d), BF16),) + tuple(
            jax.ShapeDtypeStruct((bn, nh, t, HEAD_DIM_A), dt) for dt in (BF16, F32, F32, BF16, BF16)),
        grid=(bn // bb, t // tt),
        in_specs=[pl.BlockSpec((bb, tt, d), lambda b, i: (b, i, 0)),
                  pl.BlockSpec((bb, 1, d), lambda b, i: (b, 0, 0)),
                  pl.BlockSpec((bb, 1, d), lambda b, i: (b, 0, 1)),
                  pl.BlockSpec((1, d), lambda b, i: (0, 0)),
                  _resident((d, 3 * nh * HEAD_DIM_A), lambda b, i: (0, 0)),
                  pl.BlockSpec((1, HEAD_DIM_A), lambda b, i: (0, 0)),
                  pl.BlockSpec((1, HEAD_DIM_A), lambda b, i: (0, 0))],
        out_specs=(pl.BlockSpec((bb, tt, d), lambda b, i: (b, i, 0)),) + (head_spec,) * 5,
        compiler_params=_params(("parallel", "parallel")),
        name="proj_attn",
    )(x, mod, mod, g1, w, gq, gk)


def _proj_mlstm_kernel(h_ref, wq_ref, wk_ref, wv_ref, wo_ref, wg_ref, wgt_ref, raw_ref, vo_ref, gc_ref, gr_ref):
    bb, tt, d = h_ref.shape
    c = raw_ref.shape[2]
    hb = h_ref[...].reshape(bb * tt, d)
    db = c // 2
    step = min(4 * PROJ_COLS, db)

    def w_cols(lo):
        return (wq_ref, wk_ref, wv_ref, wo_ref)[lo // db][:, lo % db:lo % db + step]

    for lo in range(0, c, step):
        raw_ref[:, :, lo:lo + step] = _dot(hb, w_cols(lo)).reshape(bb, tt, step)
        vo_ref[:, :, lo:lo + step] = _dot(hb, w_cols(c + lo)).reshape(bb, tt, step).astype(vo_ref.dtype)
    gc_ref[...] = _dot(hb, wg_ref[...]).reshape(bb, tt, LANES)
    for b in range(bb):
        gr_ref[b] = _dot_nt(wgt_ref[...], hb[b * tt:(b + 1) * tt])


def _proj_mlstm(h, w, wg, wgt):
    bn, t, d = h.shape
    db = d - d // 2
    c = 2 * db
    first = (3 * (d // 2)) // db
    bb, tt = _row_blocks(bn, t, ROW_TILE)
    return pl.pallas_call(
        _proj_mlstm_kernel,
        out_shape=(jax.ShapeDtypeStruct((bn, t, c), F32),
                   jax.ShapeDtypeStruct((bn, t, c), BF16),
                   jax.ShapeDtypeStruct((bn, t, LANES), F32),
                   jax.ShapeDtypeStruct((bn, wgt.shape[0], t), F32)),
        grid=(bn // bb, t // tt),
        in_specs=[pl.BlockSpec((bb, tt, d), lambda b, i: (b, i, 0)),
                  *[_resident((d, db), lambda b, i, j=j: (0, first + j)) for j in range(4)],
                  _resident(wg.shape, lambda b, i: (0, 0)),
                  _resident(wgt.shape, lambda b, i: (0, 0))],
        out_specs=(pl.BlockSpec((bb, tt, c), lambda b, i: (b, i, 0)),
                   pl.BlockSpec((bb, tt, c), lambda b, i: (b, i, 0)),
                   pl.BlockSpec((bb, tt, LANES), lambda b, i: (b, i, 0)),
                   pl.BlockSpec((bb, wgt.shape[0], tt), lambda b, i: (b, 0, i))),
        compiler_params=_params(("parallel", "parallel")),
        name="proj_mlstm",
    )(h, w, w, w, w, wg, wgt)


def _strict_upper_ones(n):
    r = lax.broadcasted_iota(jnp.int32, (n, n), 0)
    c = lax.broadcasted_iota(jnp.int32, (n, n), 1)
    return (r > c).astype(BF16)


def _sb_block(q, kb, vb, tri, carry, mask):
    z = _dot_nt(q, kb) * (HEAD_DIM_A ** -0.5 * LOG2_E)
    log_beta = jnp.minimum(z, 0.0) - jnp.log2(1.0 + jnp.exp2(-jnp.abs(z)))
    log_1m = log_beta - z
    if mask is not None:
        log_1m = jnp.where(mask, log_1m, 0.0)
    tail = sum(_dot(p, tri) for p in _split_bf16(log_1m, SUFFIX_SUM_TERMS))
    total = tail[:, :1] + log_1m[:, :1]
    if carry is not None:
        tail = tail + carry
    a = jnp.exp2(log_beta + tail)
    if mask is not None:
        a = jnp.where(mask, a, 0.0)
    return _dot(a.astype(BF16), vb), total


def _attn_kernel(q_ref, kd_ref, vd_ref, kp_ref, vp_ref, o_ref, *maybe_carry_ref, tk, past_is_prefix):
    hps, tq, dh = q_ref.shape[1], q_ref.shape[2], q_ref.shape[3]
    row = lax.broadcasted_iota(jnp.int32, (tq, tq), 0)
    col = lax.broadcasted_iota(jnp.int32, (tq, tq), 1)
    n_past = pl.program_id(2) * (tq // tk) if past_is_prefix else kp_ref.shape[2] // tk
    tri = _strict_upper_ones(tk)
    tri_diag = tri if tq == tk else _strict_upper_ones(tq)

    def past_block(h, jj, carry):
        start = pl.multiple_of(jnp.maximum(n_past - 1 - jj, 0) * tk, tk)
        kb = kp_ref[0, h, pl.ds(start, tk), :].astype(BF16)
        vb = vp_ref[0, h, pl.ds(start, tk), :].astype(BF16)
        return _sb_block(q_ref[0, h], kb, vb, tri, carry, None)

    state = []
    for h in range(hps):
        acc, s = _sb_block(q_ref[0, h], kd_ref[0, h].astype(BF16), vd_ref[0, h].astype(BF16), tri_diag, None,
                           col < row)
        pv, ds = past_block(h, 0, jnp.where(n_past > 0, s, -NO_PAST))
        state.append((acc + pv, s + ds))

    worst = None
    for h, (acc, s) in enumerate(state):
        def more(c):
            jj, _, s = c
            return jnp.logical_and(jj < n_past, jnp.max(s) > -F32_EXP2_UNDERFLOW)

        def body(c, h=h):
            jj, acc, s = c
            pv, ds = past_block(h, jj, s)
            return jj + 1, acc + pv, s + ds

        _, acc, s = lax.while_loop(more, body, (jnp.int32(1), acc, s))
        o_ref[0, :, h * dh:(h + 1) * dh] = acc.astype(o_ref.dtype)
        if maybe_carry_ref:
            top = jnp.max(s, axis=0, keepdims=True)
            worst = top if worst is None else jnp.maximum(worst, top)
    if maybe_carry_ref:
        carry_ref, = maybe_carry_ref
        carry_ref[0, 0, 0] = jnp.broadcast_to(worst, carry_ref.shape[3:])


def _attn_call(q, k, v, k_past, v_past, prefix, window):
    bn, nh, t, dh = q.shape
    if prefix:
        tq = tk = min(t, ATTN_BLOCK)
    else:
        tq, tk = t, min(k_past.shape[2], SAMPLE_KEY_BLOCK)
    p = k_past.shape[2] if window is None else window
    last = k_past.shape[2] // p - 1
    hps = min(nh, ATTN_HEADS_PER_STEP)
    blk = pl.BlockSpec((1, hps, tq, dh), lambda b, h, i: (b, h, i, 0))
    past = pl.BlockSpec((1, hps, p, dh), lambda b, h, i: (b, h, last, 0))
    out_shape = [jax.ShapeDtypeStruct((bn, t, nh * dh), BF16)]
    out_specs = [pl.BlockSpec((1, tq, hps * dh), lambda b, h, i: (b, i, h))]
    if window is not None:
        out_shape.append(jax.ShapeDtypeStruct((bn, nh // hps, t // tq, SUBLANES, LANES), F32))
        out_specs.append(pl.BlockSpec((1, 1, 1, SUBLANES, LANES), lambda b, h, i: (b, h, i, 0, 0)))
    return pl.pallas_call(
        functools.partial(_attn_kernel, tk=tk, past_is_prefix=prefix),
        out_shape=tuple(out_shape),
        grid=(bn, nh // hps, t // tq),
        in_specs=[blk, blk, blk, past, past],
        out_specs=tuple(out_specs),
        compiler_params=_params(("parallel", "parallel", "parallel")),
        name="attn",
    )(q, k, v, k_past, v_past)


def _attn(q, k, v, k_past, v_past):
    if k_past is None:
        return _attn_call(q, k, v, k, v, True, None)[0]
    near, carry = _attn_call(q, k, v, k_past, v_past, False, min(k_past.shape[2], SAMPLE_KEY_BLOCK))
    return lax.cond(jnp.max(carry) > -F32_EXP2_UNDERFLOW,
                    lambda: _attn_call(q, k, v, k_past, v_past, False, None)[0],
                    lambda: near)


def _mlstm_kernel(raw_ref, vo_ref, gc_ref, gr_ref, bc_ref, br_ref, gh_ref, c0_ref, n0_ref, m0_ref, prev_ref,
                  wc_ref, cb_ref, o_ref, c_out_ref, n_out_ref, m_out_ref, new_ref, c_scr, n_scr, m_scr, halo_scr):
    ci = pl.program_id(1)
    ln = raw_ref.shape[1]
    nh = N_HEADS_B
    dh = raw_ref.shape[2] // (2 * nh)

    @pl.when(ci == 0)
    def _():
        c_scr[...] = c0_ref[0]
        n_scr[...] = n0_ref[0]
        m_scr[...] = m0_ref[0]
        halo_scr[...] = prev_ref[0]

    def conv(cols, scale):
        u = raw_ref[0, :, cols]
        ext = jnp.concatenate([halo_scr[:, cols], u], axis=0)
        w = 0.5 * wc_ref[:, cols]
        h = pltpu.roll(ext, CONV_W - 1, axis=0)[SUBLANES:] * w[0:1]
        for j in range(1, CONV_W - 1):
            h = h + pltpu.roll(ext, CONV_W - 1 - j, axis=0)[SUBLANES:] * w[j:j + 1]
        h = h + u * w[CONV_W - 1:CONV_W] + 0.5 * cb_ref[:, cols]
        halo_scr[:, cols] = ext[ln:ln + SUBLANES]
        y = h + h * jnp.tanh(h)
        return (y if scale == 1.0 else y * scale).astype(BF16)

    gc = gc_ref[0] + bc_ref[...]
    gr = gr_ref[0] + br_ref[...]
    lfc = _log_sigmoid(gc)
    lfr = _log_sigmoid(gr)
    row = lax.broadcasted_iota(jnp.int32, (ln, ln), 0)
    col = lax.broadcasted_iota(jnp.int32, (ln, ln), 1)
    causal = col <= row
    ones_causal = causal.astype(BF16)
    ones_causal_t = (row <= col).astype(BF16)
    b_rows = sum(_dot(p, ones_causal_t) for p in _split_bf16(lfr, 3))

    for hd in range(nh):
        q = conv(slice(hd * dh, (hd + 1) * dh), 1.0)
        k = conv(slice((nh + hd) * dh, (nh + hd + 1) * dh), dh ** -0.5)
        v = vo_ref[0, :, hd * dh:(hd + 1) * dh]
        og = vo_ref[0, :, (nh + hd) * dh:(nh + hd + 1) * dh].astype(F32)
        ig_col = gc[:, hd:hd + 1]
        ig_row = gr[hd:hd + 1, :]
        b_row = b_rows[nh + hd:nh + hd + 1, :]
        lf_b = jnp.broadcast_to(lfc[:, nh + hd:nh + hd + 1], (ln, ln))
        bt = sum(_dot(ones_causal, p) for p in _split_bf16(lf_b, 3))
        b_col = bt[:, :1]
        m_prev = m_scr[hd][:, :1]
        c_prev = c_scr[hd]
        n_prev = n_scr[hd]

        dmat = jnp.where(causal, (bt - b_row) + ig_row, -jnp.inf)
        inter = b_col + m_prev
        m_t = jnp.maximum(inter, jnp.max(dmat, axis=1, keepdims=True))
        w_inter = jnp.exp(inter - m_t)
        s = jnp.exp(dmat - m_t) * _dot_nt(q, k)
        num = w_inter * _dot(q, c_prev.astype(BF16)) + _dot(s.astype(BF16), v)
        den = (w_inter * jnp.sum(q.astype(F32) * n_prev, axis=1, keepdims=True)
               + jnp.sum(s, axis=1, keepdims=True))
        h = num / jnp.maximum(jnp.abs(den), jnp.exp(-m_t))
        g = h * lax.rsqrt(jnp.mean(h * h, axis=1, keepdims=True) + EPS) * (0.5 * gh_ref[hd])
        o_ref[0, :, hd * dh:(hd + 1) * dh] = (g + g * jnp.tanh(0.5 * og)).astype(o_ref.dtype)

        b_last = b_col[ln - 1:ln]
        m_new = m_t[ln - 1:ln]
        w_state = jnp.exp((b_last - b_col) + ig_col - m_new)
        decay = jnp.exp(b_last + m_prev - m_new)
        ks = k.astype(F32) * w_state
        c_scr[hd] = decay * c_prev + _dot_tn(ks.astype(BF16), v)
        n_scr[hd] = decay * n_prev + jnp.sum(ks, axis=0, keepdims=True)
        m_scr[hd] = jnp.broadcast_to(m_new, (1, LANES))

    @pl.when(ci == pl.num_programs(1) - 1)
    def _():
        c_out_ref[0] = c_scr[...]
        n_out_ref[0] = n_scr[...]
        m_out_ref[0] = m_scr[...]
        new_ref[0] = halo_scr[SUBLANES - (CONV_W - 1):SUBLANES]


def _mlstm(raw, vo, gc, gr, bias_c, bias_r, gh, c0, n0, m0, prev8, wc, cb):
    bn, t, c2 = raw.shape
    nh = N_HEADS_B
    dh = c2 // (2 * nh)
    ln = min(t, MLSTM_CHUNK)
    rows = gr.shape[1]
    state = lambda shape: pl.BlockSpec((1,) + shape, lambda b, i: (b, 0, 0, 0))
    return pl.pallas_call(
        _mlstm_kernel,
        out_shape=(jax.ShapeDtypeStruct((bn, t, nh * dh), BF16),
                   jax.ShapeDtypeStruct((bn, nh, dh, dh), F32),
                   jax.ShapeDtypeStruct((bn, nh, 1, dh), F32),
                   jax.ShapeDtypeStruct((bn, nh, 1, LANES), F32),
                   jax.ShapeDtypeStruct((bn, CONV_W - 1, c2), F32)),
        grid=(bn, t // ln),
        in_specs=[pl.BlockSpec((1, ln, c2), lambda b, i: (b, i, 0)),
                  pl.BlockSpec((1, ln, c2), lambda b, i: (b, i, 0)),
                  pl.BlockSpec((1, ln, LANES), lambda b, i: (b, i, 0)),
                  pl.BlockSpec((1, rows, ln), lambda b, i: (b, 0, i)),
                  pl.BlockSpec((1, LANES), lambda b, i: (0, 0)),
                  pl.BlockSpec((rows, 1), lambda b, i: (0, 0)),
                  pl.BlockSpec((nh, 1, dh), lambda b, i: (0, 0, 0)),
                  state((nh, dh, dh)), state((nh, 1, dh)), state((nh, 1, LANES)),
                  pl.BlockSpec((1, SUBLANES, c2), lambda b, i: (b, 0, 0)),
                  pl.BlockSpec((CONV_W, c2), lambda b, i: (0, 0)),
                  pl.BlockSpec((1, c2), lambda b, i: (0, 0))],
        out_specs=(pl.BlockSpec((1, ln, nh * dh), lambda b, i: (b, i, 0)),
                   state((nh, dh, dh)), state((nh, 1, dh)), state((nh, 1, LANES)),
                   pl.BlockSpec((1, CONV_W - 1, c2), lambda b, i: (b, 0, 0))),
        scratch_shapes=[pltpu.VMEM((nh, dh, dh), F32), pltpu.VMEM((nh, 1, dh), F32),
                        pltpu.VMEM((nh, 1, LANES), F32), pltpu.VMEM((SUBLANES, c2), F32)],
        compiler_params=_params(("parallel", "arbitrary")),
        name="mlstm",
    )(raw, vo, gc, gr, bias_c, bias_r, gh, c0, n0, m0, prev8, wc, cb)


def _out_proj_kernel(x_ref, oa_ref, ob_ref, gt_ref, w_ref, sh_ref, sc_ref, g2_ref, o_ref, h_ref):
    bb, tt, d = x_ref.shape
    da, db = oa_ref.shape[2], ob_ref.shape[2]
    y = (_dot(oa_ref[...].reshape(bb * tt, da), w_ref[0:da])
         + _dot(ob_ref[...].reshape(bb * tt, db), w_ref[da:da + db]))
    x1 = x_ref[...] + gt_ref[...] * y.reshape(bb, tt, d)
    o_ref[...] = x1
    h_ref[...] = _modulated_norm(x1, g2_ref[...], sc_ref[...], sh_ref[...]).astype(BF16)


def _out_proj(x, oa, ob, mod, w, g2):
    bn, t, d = x.shape
    bb, tt = _row_blocks(bn, t, ROW_TILE)
    rows = lambda width: pl.BlockSpec((bb, tt, width), lambda b, i: (b, i, 0))
    return pl.pallas_call(
        _out_proj_kernel,
        out_shape=(jax.ShapeDtypeStruct((bn, t, d), F32), jax.ShapeDtypeStruct((bn, t, d), BF16)),
        grid=(bn // bb, t // tt),
        in_specs=[rows(d), rows(oa.shape[2]), rows(ob.shape[2]),
                  pl.BlockSpec((bb, 1, d), lambda b, i: (b, 0, 2)),
                  _resident(w.shape, lambda b, i: (0, 0)),
                  pl.BlockSpec((bb, 1, d), lambda b, i: (b, 0, 3)),
                  pl.BlockSpec((bb, 1, d), lambda b, i: (b, 0, 4)),
                  pl.BlockSpec((1, d), lambda b, i: (0, 0))],
        out_specs=(rows(d), rows(d)),
        compiler_params=_params(("parallel", "parallel")),
        name="out_proj",
    )(x, oa, ob, mod, w, mod, mod, g2)


def _ffn_kernel(x_ref, h_ref, gt_ref, w1_ref, w2_ref, o_ref, acc_scr):
    f = pl.program_id(2)
    bb, tt, d = x_ref.shape

    a = jnp.maximum(_dot(h_ref[...].reshape(bb * tt, d), w1_ref[...]), 0.0)
    acc = jnp.where(f == 0, 0.0, acc_scr[...]) + _dot((a * a).astype(BF16), w2_ref[...])
    acc_scr[...] = acc
    o_ref[...] = x_ref[...] + gt_ref[...] * acc.reshape(bb, tt, d)


def _ffn(x, h, mod, w1, w2):
    bn, t, d = x.shape
    dff = w1.shape[1]
    bb, tt = _row_blocks(bn, t, ROW_TILE)
    rows = pl.BlockSpec((bb, tt, d), lambda b, i, f: (b, i, 0))
    chunk = lambda c: pl.BlockSpec((bb, 1, d), lambda b, i, f: (b, 0, c))
    return pl.pallas_call(
        _ffn_kernel,
        out_shape=jax.ShapeDtypeStruct((bn, t, d), F32),
        grid=(bn // bb, t // tt, dff // FF_TILE),
        in_specs=[rows, rows, chunk(5),
                  pl.BlockSpec((d, FF_TILE), lambda b, i, f: (0, f)),
                  pl.BlockSpec((FF_TILE, d), lambda b, i, f: (f, 0))],
        out_specs=rows,
        scratch_shapes=[pltpu.VMEM((bb * tt, d), F32)],
        compiler_params=_params(("parallel", "parallel", "arbitrary")),
        name="ffn",
    )(x, h, mod, w1, w2)


def _layer(x, mod, past, wts):
    bn, t, d = x.shape
    db = d - d // 2
    nh, dh = N_HEADS_B, db // N_HEADS_B
    if past is None:
        k_past = v_past = None
        conv_prev = jnp.zeros((bn, CONV_W - 1, 2 * db), F32)
        c0 = jnp.zeros((bn, nh, dh, dh), F32)
        n0 = jnp.zeros((bn, nh, dh), F32)
        m0 = jnp.zeros((bn, nh), F32)
    else:
        k_past, v_past, c0, n0, m0, conv_prev = (a.astype(F32) for a in past)

    h, q, k, v, k_bf, v_bf = _proj_attn(x, mod, wts["g_norm1"], wts["w_in"], wts["g_q"], wts["g_k"])
    prev8 = jnp.pad(conv_prev, ((0, 0), (SUBLANES - (CONV_W - 1), 0), (0, 0)))
    raw, vo, gc, gr = _proj_mlstm(h, wts["w_in"], wts["w_gate"], wts["w_gate_t"])
    o_a = _attn(q, k_bf, v_bf, k_past, v_past)
    o_b, c_new, n_new, m_new, conv_new = _mlstm(
        raw, vo, gc, gr, wts["gate_bias_c"], wts["gate_bias_r"], wts["g_h"], c0, n0.reshape(bn, nh, 1, dh),
        jnp.broadcast_to(m0[:, :, None, None], (bn, nh, 1, LANES)), prev8, wts["w_conv"], wts["b_conv"])

    x, h2 = _out_proj(x, o_a, o_b, mod, wts["w_out"], wts["g_norm2"])
    x = _ffn(x, h2, mod, wts["w_ff1"], wts["w_ff2"])
    return x, (k, v, c_new, n_new.reshape(bn, nh, dh), m_new[:, :, 0, 0], conv_new)


def _layer_weights(l, w_in, g_norm1, g_q, g_k, w_conv, b_conv, b_i, b_f, g_h, w_out, g_norm2, w_ff1, w_ff2):
    d = w_in.shape[1]
    nh = N_HEADS_B
    w_gate = w_in[l][:, w_in.shape[2] - 2 * nh:]
    bias = jnp.concatenate([b_i[l], b_f[l]])
    rows = 2 * SUBLANES
    return {
        "w_in": w_in[l].astype(BF16),
        "w_gate": jnp.pad(w_gate, ((0, 0), (0, LANES - 2 * nh))).astype(BF16),
        "w_gate_t": jnp.pad(w_gate.T, ((0, rows - 2 * nh), (0, 0))).astype(BF16),
        "gate_bias_c": jnp.pad(bias, (0, LANES - 2 * nh)).reshape(1, LANES),
        "gate_bias_r": jnp.pad(bias, (0, rows - 2 * nh)).reshape(rows, 1),
        "g_norm1": g_norm1[l].reshape(1, d), "g_norm2": g_norm2[l].reshape(1, d),
        "g_q": g_q[l].reshape(1, -1), "g_k": g_k[l].reshape(1, -1),
        "w_conv": w_conv[l], "b_conv": b_conv[l].reshape(1, -1),
        "g_h": g_h[l].reshape(nh, 1, -1),
        "w_out": w_out[l].astype(BF16), "w_ff1": w_ff1[l].astype(BF16), "w_ff2": w_ff2[l].astype(BF16),
    }


def kernel(x_prompt, x_sample, c_prompt, c_sample, cache_k, cache_v, state_C, state_n, state_m, state_conv,
           w_ada, b_ada, g_norm1, w_in, g_q, g_k, w_conv, b_conv, b_i, b_f, g_h, w_out, g_norm2, w_ff1, w_ff2):
    depth = w_ada.shape[0]
    bp, bs = c_prompt.shape[0], c_sample.shape[0]
    c_rows = -(-(bp + bs) // (2 * SUBLANES)) * (2 * SUBLANES)
    c_all = jnp.pad(jnp.concatenate([c_prompt, c_sample], axis=0), ((0, c_rows - bp - bs), (0, 0)))
    y_prompt, y_sample = x_prompt, x_sample
    new_p, new_s = [], []
    for l in range(depth):
        wts = _layer_weights(l, w_in, g_norm1, g_q, g_k, w_conv, b_conv, b_i, b_f, g_h, w_out, g_norm2,
                             w_ff1, w_ff2)
        mod = _ada(c_all, w_ada[l], b_ada[l].reshape(1, -1))
        mod_p = mod[:bp].reshape(bp, 1, -1)
        mod_s = mod[bp:bp + bs].reshape(bs, 1, -1)
        y_prompt, sp = _layer(y_prompt, mod_p, None, wts)
        past = (cache_k[l], cache_v[l], state_C[l], state_n[l], state_m[l], state_conv[l])
        y_sample, ss = _layer(y_sample, mod_s, past, wts)
        new_p.append(sp)
        new_s.append(ss)
    stack = lambda states, i: jnp.stack([s[i] for s in states], axis=0)
    return ((y_prompt, y_sample) + tuple(stack(new_p, i) for i in range(6))
            + tuple(stack(new_s, i) for i in range(6)))
```

```python
import functools

import jax
import jax.numpy as jnp
from jax import lax
from jax.experimental import pallas as pl
from jax.experimental.pallas import tpu as pltpu

F32, BF16 = jnp.float32, jnp.bfloat16
EPS = 1e-6
HEAD_DIM_A = 128
N_HEADS_B = 4
CONV_W = 4
LANES = 128
SUBLANES = 8
F32_EXP2_UNDERFLOW = 151.0
LOG2_E = 1.4426950408889634
NO_PAST = 1e30
V7X_VMEM_BYTES = 64 * 2 ** 20
VMEM_LIMIT = V7X_VMEM_BYTES * 7 // 8

ROW_TILE = 512
ATTN_BLOCK = 256
SAMPLE_KEY_BLOCK = 256
MLSTM_CHUNK = 256
FF_TILE = 1024
ADA_TILE = 1024
ATTN_HEADS_PER_STEP = 4
SUFFIX_SUM_TERMS = 1
PROJ_COLS = 256


def _params(semantics, vmem_limit=VMEM_LIMIT):
    return pltpu.CompilerParams(dimension_semantics=semantics, vmem_limit_bytes=vmem_limit)


def _resident(shape, index_map):
    return pl.BlockSpec(shape, index_map, pipeline_mode=pl.Buffered(1))


def _sigmoid(x):
    return 0.5 * jnp.tanh(0.5 * x) + 0.5


def _log_sigmoid(x):
    return jnp.minimum(x, 0.0) - jnp.log1p(jnp.exp(-jnp.abs(x)))


def _dot(a, b):
    return jnp.dot(a, b, preferred_element_type=F32)


def _dot_nt(a, b):
    return lax.dot_general(a, b, (((1,), (1,)), ((), ())), preferred_element_type=F32)


def _dot_tn(a, b):
    return lax.dot_general(a, b, (((0,), (0,)), ((), ())), preferred_element_type=F32)


def _split_bf16(x, parts):
    out = []
    for _ in range(parts - 1):
        p = x.astype(BF16)
        out.append(p)
        x = x - p.astype(F32)
    out.append(x.astype(BF16))
    return out


def _modulated_norm(x, g, sc, sh):
    y = x * lax.rsqrt(jnp.mean(x * x, axis=-1, keepdims=True) + EPS)
    return (y * g) * (1.0 + sc) + sh


def _row_blocks(bn, t, rows):
    if t >= rows:
        return 1, rows
    return min(bn, rows // t), t


def _ada_kernel(c_ref, w_ref, b_ref, o_ref):
    c = c_ref[...]
    s = (c * _sigmoid(c)).astype(BF16)
    o_ref[...] = _dot(s, w_ref[...].astype(BF16)) + b_ref[...]


def _ada(c, w, b):
    m, d = c.shape
    n = w.shape[1]
    return pl.pallas_call(
        _ada_kernel,
        out_shape=jax.ShapeDtypeStruct((m, n), F32),
        grid=(n // ADA_TILE,),
        in_specs=[pl.BlockSpec((m, d), lambda j: (0, 0)),
                  pl.BlockSpec((d, ADA_TILE), lambda j: (0, j)),
                  pl.BlockSpec((1, ADA_TILE), lambda j: (0, j))],
        out_specs=pl.BlockSpec((m, ADA_TILE), lambda j: (0, j)),
        compiler_params=_params(("parallel",)),
        name="ada",
    )(c, w, b)


def _proj_attn_kernel(x_ref, sh_ref, sc_ref, g1_ref, w_ref, gq_ref, gk_ref, h_ref, q_ref, k_ref, v_ref, kb_ref,
                      vb_ref):
    bb, tt, d = x_ref.shape
    nh = q_ref.shape[1]
    da = nh * HEAD_DIM_A
    h3 = _modulated_norm(x_ref[...], g1_ref[...], sc_ref[...], sh_ref[...]).astype(BF16)
    h_ref[...] = h3
    hb = h3.reshape(bb * tt, d)
    outs = (((q_ref,), gq_ref), ((k_ref, kb_ref), gk_ref), ((v_ref, vb_ref), None))
    for part, (out_refs, g_ref) in enumerate(outs):
        u = _dot_nt(hb, w_ref[part * da:(part + 1) * da, :])
        for hd in range(nh):
            uh = u[:, hd * HEAD_DIM_A:(hd + 1) * HEAD_DIM_A]
            if g_ref is not None:
                uh = uh * lax.rsqrt(jnp.mean(uh * uh, axis=-1, keepdims=True) + EPS) * g_ref[...]
            for out_ref in out_refs:
                out_ref[:, hd] = uh.reshape(bb, tt, HEAD_DIM_A).astype(out_ref.dtype)


def _proj_attn(x, mod, g1, w, gq, gk):
    bn, t, d = x.shape
    nh = d // 2 // HEAD_DIM_A
    bb, tt = _row_blocks(bn, t, ROW_TILE)
    head_spec = pl.BlockSpec((bb, nh, tt, HEAD_DIM_A), lambda b, i: (b, 0, i, 0))
    return pl.pallas_call(
        _proj_attn_kernel,
        out_shape=(jax.ShapeDtypeStruct((bn, t, d), BF16),) + tuple(
            jax.ShapeDtypeStruct((bn, nh, t, HEAD_DIM_A), dt) for dt in (BF16, F32, F32, BF16, BF16)),
        grid=(bn // bb, t // tt),
        in_specs=[pl.BlockSpec((bb, tt, d), lambda b, i: (b, i, 0)),
                  pl.BlockSpec((bb, 1, d), lambda b, i: (b, 0, 0)),
                  pl.BlockSpec((bb, 1, d), lambda b, i: (b, 0, 1)),
                  pl.BlockSpec((1, d), lambda b, i: (0, 0)),
                  _resident((3 * nh * HEAD_DIM_A, d), lambda b, i: (0, 0)),
                  pl.BlockSpec((1, HEAD_DIM_A), lambda b, i: (0, 0)),
                  pl.BlockSpec((1, HEAD_DIM_A), lambda b, i: (0, 0))],
        out_specs=(pl.BlockSpec((bb, tt, d), lambda b, i: (b, i, 0)),) + (head_spec,) * 5,
        compiler_params=_params(("parallel", "parallel")),
        name="proj_attn",
    )(x, mod, mod, g1, w, gq, gk)


def _proj_mlstm_kernel(h_ref, wq_ref, wk_ref, wv_ref, wo_ref, wg_ref, wgt_ref, raw_ref, vo_ref, gc_ref, gr_ref):
    bb, tt, d = h_ref.shape
    c = raw_ref.shape[2]
    hb = h_ref[...].reshape(bb * tt, d)
    db = c // 2
    step = min(4 * PROJ_COLS, db)

    def w_cols(lo):
        return (wq_ref, wk_ref, wv_ref, wo_ref)[lo // db][lo % db:lo % db + step, :]

    for lo in range(0, c, step):
        raw_ref[:, :, lo:lo + step] = _dot_nt(hb, w_cols(lo)).reshape(bb, tt, step)
        vo_ref[:, :, lo:lo + step] = _dot_nt(hb, w_cols(c + lo)).reshape(bb, tt, step).astype(vo_ref.dtype)
    gc_ref[...] = _dot(hb, wg_ref[...]).reshape(bb, tt, LANES)
    for b in range(bb):
        gr_ref[b] = _dot_nt(wgt_ref[...], hb[b * tt:(b + 1) * tt])


def _proj_mlstm(h, w, wg, wgt):
    bn, t, d = h.shape
    db = d - d // 2
    c = 2 * db
    first = (3 * (d // 2)) // db
    bb, tt = _row_blocks(bn, t, ROW_TILE)
    return pl.pallas_call(
        _proj_mlstm_kernel,
        out_shape=(jax.ShapeDtypeStruct((bn, t, c), F32),
                   jax.ShapeDtypeStruct((bn, t, c), BF16),
                   jax.ShapeDtypeStruct((bn, t, LANES), F32),
                   jax.ShapeDtypeStruct((bn, wgt.shape[0], t), F32)),
        grid=(bn // bb, t // tt),
        in_specs=[pl.BlockSpec((bb, tt, d), lambda b, i: (b, i, 0)),
                  *[_resident((db, d), lambda b, i, j=j: (first + j, 0)) for j in range(4)],
                  _resident(wg.shape, lambda b, i: (0, 0)),
                  _resident(wgt.shape, lambda b, i: (0, 0))],
        out_specs=(pl.BlockSpec((bb, tt, c), lambda b, i: (b, i, 0)),
                   pl.BlockSpec((bb, tt, c), lambda b, i: (b, i, 0)),
                   pl.BlockSpec((bb, tt, LANES), lambda b, i: (b, i, 0)),
                   pl.BlockSpec((bb, wgt.shape[0], tt), lambda b, i: (b, 0, i))),
        compiler_params=_params(("parallel", "parallel")),
        name="proj_mlstm",
    )(h, w, w, w, w, wg, wgt)


def _strict_lower_ones(n):
    r = lax.broadcasted_iota(jnp.int32, (n, n), 0)
    c = lax.broadcasted_iota(jnp.int32, (n, n), 1)
    return (r > c).astype(BF16)


def _sb_block(q, kb, vb, tri, carry, mask):
    z = _dot_nt(q, kb) * (HEAD_DIM_A ** -0.5 * LOG2_E)
    log_beta = jnp.minimum(z, 0.0) - jnp.log2(1.0 + jnp.exp2(-jnp.abs(z)))
    log_1m = log_beta - z
    if mask is not None:
        log_1m = jnp.where(mask, log_1m, 0.0)
    tail = sum(_dot(p, tri) for p in _split_bf16(log_1m, SUFFIX_SUM_TERMS))
    total = tail[:, :1] + log_1m[:, :1]
    if carry is not None:
        tail = tail + carry
    a = jnp.exp2(log_beta + tail)
    if mask is not None:
        a = jnp.where(mask, a, 0.0)
    return _dot(a.astype(BF16), vb), total


def _attn_kernel(q_ref, kd_ref, vd_ref, kp_ref, vp_ref, o_ref, *maybe_carry_ref, tk, past_is_prefix):
    hps, tq, dh = q_ref.shape[1], q_ref.shape[2], q_ref.shape[3]
    row = lax.broadcasted_iota(jnp.int32, (tq, tq), 0)
    col = lax.broadcasted_iota(jnp.int32, (tq, tq), 1)
    n_past = pl.program_id(2) * (tq // tk) if past_is_prefix else kp_ref.shape[2] // tk
    tri = _strict_lower_ones(tk)
    tri_diag = tri if tq == tk else _strict_lower_ones(tq)

    def past_block(h, jj, carry):
        start = pl.multiple_of(jnp.maximum(n_past - 1 - jj, 0) * tk, tk)
        kb = kp_ref[0, h, pl.ds(start, tk), :].astype(BF16)
        vb = vp_ref[0, h, pl.ds(start, tk), :].astype(BF16)
        return _sb_block(q_ref[0, h], kb, vb, tri, carry, None)

    state = []
    for h in range(hps):
        acc, s = _sb_block(q_ref[0, h], kd_ref[0, h].astype(BF16), vd_ref[0, h].astype(BF16), tri_diag, None,
                           col < row)
        pv, ds = past_block(h, 0, jnp.where(n_past > 0, s, -NO_PAST))
        state.append((acc + pv, s + ds))

    worst = None
    for h, (acc, s) in enumerate(state):
        def more(c):
            jj, _, s = c
            return jnp.logical_and(jj < n_past, jnp.max(s) > -F32_EXP2_UNDERFLOW)

        def body(c, h=h):
            jj, acc, s = c
            pv, ds = past_block(h, jj, s)
            return jj + 1, acc + pv, s + ds

        _, acc, s = lax.while_loop(more, body, (jnp.int32(1), acc, s))
        o_ref[0, :, h * dh:(h + 1) * dh] = acc.astype(o_ref.dtype)
        if maybe_carry_ref:
            top = jnp.max(s, axis=0, keepdims=True)
            worst = top if worst is None else jnp.maximum(worst, top)
    if maybe_carry_ref:
        carry_ref, = maybe_carry_ref
        carry_ref[0, 0, 0] = jnp.broadcast_to(worst, carry_ref.shape[3:])


def _attn_call(q, k, v, k_past, v_past, prefix, window):
    bn, nh, t, dh = q.shape
    if prefix:
        tq = tk = min(t, ATTN_BLOCK)
    else:
        tq, tk = t, min(k_past.shape[2], SAMPLE_KEY_BLOCK)
    p = k_past.shape[2] if window is None else window
    last = k_past.shape[2] // p - 1
    hps = min(nh, ATTN_HEADS_PER_STEP)
    blk = pl.BlockSpec((1, hps, tq, dh), lambda b, h, i: (b, h, i, 0))
    past = pl.BlockSpec((1, hps, p, dh), lambda b, h, i: (b, h, last, 0))
    out_shape = [jax.ShapeDtypeStruct((bn, t, nh * dh), BF16)]
    out_specs = [pl.BlockSpec((1, tq, hps * dh), lambda b, h, i: (b, i, h))]
    if window is not None:
        out_shape.append(jax.ShapeDtypeStruct((bn, nh // hps, t // tq, SUBLANES, LANES), F32))
        out_specs.append(pl.BlockSpec((1, 1, 1, SUBLANES, LANES), lambda b, h, i: (b, h, i, 0, 0)))
    return pl.pallas_call(
        functools.partial(_attn_kernel, tk=tk, past_is_prefix=prefix),
        out_shape=tuple(out_shape),
        grid=(bn, nh // hps, t // tq),
        in_specs=[blk, blk, blk, past, past],
        out_specs=tuple(out_specs),
        compiler_params=_params(("parallel", "parallel", "parallel")),
        name="attn",
    )(q, k, v, k_past, v_past)


def _attn(q, k, v, k_past, v_past):
    if k_past is None:
        return _attn_call(q, k, v, k, v, True, None)[0]
    near, carry = _attn_call(q, k, v, k_past, v_past, False, min(k_past.shape[2], SAMPLE_KEY_BLOCK))
    return lax.cond(jnp.max(carry) > -F32_EXP2_UNDERFLOW,
                    lambda: _attn_call(q, k, v, k_past, v_past, False, None)[0],
                    lambda: near)


def _mlstm_kernel(raw_ref, vo_ref, gc_ref, gr_ref, bc_ref, br_ref, gh_ref, c0_ref, n0_ref, m0_ref, prev_ref,
                  wc_ref, cb_ref, o_ref, c_out_ref, n_out_ref, m_out_ref, new_ref, c_scr, n_scr, m_scr, halo_scr):
    ci = pl.program_id(1)
    ln = raw_ref.shape[1]
    nh = N_HEADS_B
    dh = raw_ref.shape[2] // (2 * nh)

    @pl.when(ci == 0)
    def _():
        c_scr[...] = c0_ref[0]
        n_scr[...] = n0_ref[0]
        m_scr[...] = m0_ref[0]
        halo_scr[...] = prev_ref[0]

    def conv(cols, scale):
        u = raw_ref[0, :, cols]
        ext = jnp.concatenate([halo_scr[:, cols], u], axis=0)
        w = 0.5 * wc_ref[:, cols]
        h = pltpu.roll(ext, CONV_W - 1, axis=0)[SUBLANES:] * w[0:1]
        for j in range(1, CONV_W - 1):
            h = h + pltpu.roll(ext, CONV_W - 1 - j, axis=0)[SUBLANES:] * w[j:j + 1]
        h = h + u * w[CONV_W - 1:CONV_W] + 0.5 * cb_ref[:, cols]
        halo_scr[:, cols] = ext[ln:ln + SUBLANES]
        y = h + h * jnp.tanh(h)
        return (y if scale == 1.0 else y * scale).astype(BF16)

    gc = gc_ref[0] + bc_ref[...]
    gr = gr_ref[0] + br_ref[...]
    lfc = _log_sigmoid(gc)
    lfr = _log_sigmoid(gr)
    row = lax.broadcasted_iota(jnp.int32, (ln, ln), 0)
    col = lax.broadcasted_iota(jnp.int32, (ln, ln), 1)
    causal = col <= row
    ones_causal = causal.astype(BF16)
    ones_causal_t = (row <= col).astype(BF16)
    b_rows = sum(_dot(p, ones_causal_t) for p in _split_bf16(lfr, 3))

    for hd in range(nh):
        q = conv(slice(hd * dh, (hd + 1) * dh), 1.0)
        k = conv(slice((nh + hd) * dh, (nh + hd + 1) * dh), dh ** -0.5)
        v = vo_ref[0, :, hd * dh:(hd + 1) * dh]
        og = vo_ref[0, :, (nh + hd) * dh:(nh + hd + 1) * dh].astype(F32)
        ig_col = gc[:, hd:hd + 1]
        ig_row = gr[hd:hd + 1, :]
        b_row = b_rows[nh + hd:nh + hd + 1, :]
        lf_b = jnp.broadcast_to(lfc[:, nh + hd:nh + hd + 1], (ln, ln))
        bt = sum(_dot(ones_causal, p) for p in _split_bf16(lf_b, 3))
        b_col = bt[:, :1]
        m_prev = m_scr[hd][:, :1]
        c_prev = c_scr[hd]
        n_prev = n_scr[hd]

        dmat = jnp.where(causal, (bt - b_row) + ig_row, -jnp.inf)
        inter = b_col + m_prev
        m_t = jnp.maximum(inter, jnp.max(dmat, axis=1, keepdims=True))
        w_inter = jnp.exp(inter - m_t)
        s = jnp.exp(dmat - m_t) * _dot_nt(q, k)
        num = w_inter * _dot(q, c_prev.astype(BF16)) + _dot(s.astype(BF16), v)
        den = (w_inter * jnp.sum(q.astype(F32) * n_prev, axis=1, keepdims=True)
               + jnp.sum(s, axis=1, keepdims=True))
        h = num / jnp.maximum(jnp.abs(den), jnp.exp(-m_t))
        g = h * lax.rsqrt(jnp.mean(h * h, axis=1, keepdims=True) + EPS) * (0.5 * gh_ref[hd])
        o_ref[0, :, hd * dh:(hd + 1) * dh] = (g + g * jnp.tanh(0.5 * og)).astype(o_ref.dtype)

        b_last = b_col[ln - 1:ln]
        m_new = m_t[ln - 1:ln]
        w_state = jnp.exp((b_last - b_col) + ig_col - m_new)
        decay = jnp.exp(b_last + m_prev - m_new)
        ks = k.astype(F32) * w_state
        c_scr[hd] = decay * c_prev + _dot_tn(ks.astype(BF16), v)
        n_scr[hd] = decay * n_prev + jnp.sum(ks, axis=0, keepdims=True)
        m_scr[hd] = jnp.broadcast_to(m_new, (1, LANES))

    @pl.when(ci == pl.num_programs(1) - 1)
    def _():
        c_out_ref[0] = c_scr[...]
        n_out_ref[0] = n_scr[...]
        m_out_ref[0] = m_scr[...]
        new_ref[0] = halo_scr[SUBLANES - (CONV_W - 1):SUBLANES]


def _mlstm(raw, vo, gc, gr, bias_c, bias_r, gh, c0, n0, m0, prev8, wc, cb):
    bn, t, c2 = raw.shape
    nh = N_HEADS_B
    dh = c2 // (2 * nh)
    ln = min(t, MLSTM_CHUNK)
    rows = gr.shape[1]
    state = lambda shape: pl.BlockSpec((1,) + shape, lambda b, i: (b, 0, 0, 0))
    return pl.pallas_call(
        _mlstm_kernel,
        out_shape=(jax.ShapeDtypeStruct((bn, t, nh * dh), BF16),
                   jax.ShapeDtypeStruct((bn, nh, dh, dh), F32),
                   jax.ShapeDtypeStruct((bn, nh, 1, dh), F32),
                   jax.ShapeDtypeStruct((bn, nh, 1, LANES), F32),
                   jax.ShapeDtypeStruct((bn, CONV_W - 1, c2), F32)),
        grid=(bn, t // ln),
        in_specs=[pl.BlockSpec((1, ln, c2), lambda b, i: (b, i, 0)),
                  pl.BlockSpec((1, ln, c2), lambda b, i: (b, i, 0)),
                  pl.BlockSpec((1, ln, LANES), lambda b, i: (b, i, 0)),
                  pl.BlockSpec((1, rows, ln), lambda b, i: (b, 0, i)),
                  pl.BlockSpec((1, LANES), lambda b, i: (0, 0)),
                  pl.BlockSpec((rows, 1), lambda b, i: (0, 0)),
                  pl.BlockSpec((nh, 1, dh), lambda b, i: (0, 0, 0)),
                  state((nh, dh, dh)), state((nh, 1, dh)), state((nh, 1, LANES)),
                  pl.BlockSpec((1, SUBLANES, c2), lambda b, i: (b, 0, 0)),
                  pl.BlockSpec((CONV_W, c2), lambda b, i: (0, 0)),
                  pl.BlockSpec((1, c2), lambda b, i: (0, 0))],
        out_specs=(pl.BlockSpec((1, ln, nh * dh), lambda b, i: (b, i, 0)),
                   state((nh, dh, dh)), state((nh, 1, dh)), state((nh, 1, LANES)),
                   pl.BlockSpec((1, CONV_W - 1, c2), lambda b, i: (b, 0, 0))),
        scratch_shapes=[pltpu.VMEM((nh, dh, dh), F32), pltpu.VMEM((nh, 1, dh), F32),
                        pltpu.VMEM((nh, 1, LANES), F32), pltpu.VMEM((SUBLANES, c2), F32)],
        compiler_params=_params(("parallel", "arbitrary")),
        name="mlstm",
    )(raw, vo, gc, gr, bias_c, bias_r, gh, c0, n0, m0, prev8, wc, cb)


def _out_proj_kernel(x_ref, oa_ref, ob_ref, gt_ref, w_ref, sh_ref, sc_ref, g2_ref, o_ref, h_ref):
    bb, tt, d = x_ref.shape
    da, db = oa_ref.shape[2], ob_ref.shape[2]
    y = (_dot(oa_ref[...].reshape(bb * tt, da), w_ref[0:da])
         + _dot(ob_ref[...].reshape(bb * tt, db), w_ref[da:da + db]))
    x1 = x_ref[...] + gt_ref[...] * y.reshape(bb, tt, d)
    o_ref[...] = x1
    h_ref[...] = _modulated_norm(x1, g2_ref[...], sc_ref[...], sh_ref[...]).astype(BF16)


def _out_proj(x, oa, ob, mod, w, g2):
    bn, t, d = x.shape
    bb, tt = _row_blocks(bn, t, ROW_TILE)
    rows = lambda width: pl.BlockSpec((bb, tt, width), lambda b, i: (b, i, 0))
    return pl.pallas_call(
        _out_proj_kernel,
        out_shape=(jax.ShapeDtypeStruct((bn, t, d), F32), jax.ShapeDtypeStruct((bn, t, d), BF16)),
        grid=(bn // bb, t // tt),
        in_specs=[rows(d), rows(oa.shape[2]), rows(ob.shape[2]),
                  pl.BlockSpec((bb, 1, d), lambda b, i: (b, 0, 2)),
                  _resident(w.shape, lambda b, i: (0, 0)),
                  pl.BlockSpec((bb, 1, d), lambda b, i: (b, 0, 3)),
                  pl.BlockSpec((bb, 1, d), lambda b, i: (b, 0, 4)),
                  pl.BlockSpec((1, d), lambda b, i: (0, 0))],
        out_specs=(rows(d), rows(d)),
        compiler_params=_params(("parallel", "parallel")),
        name="out_proj",
    )(x, oa, ob, mod, w, mod, mod, g2)


def _ffn_kernel(x_ref, h_ref, gt_ref, w1_ref, w2_ref, o_ref, acc_scr):
    f = pl.program_id(2)
    bb, tt, d = x_ref.shape

    a = jnp.maximum(_dot(h_ref[...].reshape(bb * tt, d), w1_ref[...]), 0.0)
    acc = jnp.where(f == 0, 0.0, acc_scr[...]) + _dot((a * a).astype(BF16), w2_ref[...])
    acc_scr[...] = acc
    o_ref[...] = x_ref[...] + gt_ref[...] * acc.reshape(bb, tt, d)


def _ffn(x, h, mod, w1, w2):
    bn, t, d = x.shape
    dff = w1.shape[1]
    bb, tt = _row_blocks(bn, t, ROW_TILE)
    rows = pl.BlockSpec((bb, tt, d), lambda b, i, f: (b, i, 0))
    chunk = lambda c: pl.BlockSpec((bb, 1, d), lambda b, i, f: (b, 0, c))
    return pl.pallas_call(
        _ffn_kernel,
        out_shape=jax.ShapeDtypeStruct((bn, t, d), F32),
        grid=(bn // bb, t // tt, dff // FF_TILE),
        in_specs=[rows, rows, chunk(5),
                  pl.BlockSpec((d, FF_TILE), lambda b, i, f: (0, f)),
                  pl.BlockSpec((FF_TILE, d), lambda b, i, f: (f, 0))],
        out_specs=rows,
        scratch_shapes=[pltpu.VMEM((bb * tt, d), F32)],
        compiler_params=_params(("parallel", "parallel", "arbitrary")),
        name="ffn",
    )(x, h, mod, w1, w2)


def _layer(x, mod, past, wts):
    bn, t, d = x.shape
    db = d - d // 2
    nh, dh = N_HEADS_B, db // N_HEADS_B
    if past is None:
        k_past = v_past = None
        conv_prev = jnp.zeros((bn, CONV_W - 1, 2 * db), F32)
        c0 = jnp.zeros((bn, nh, dh, dh), F32)
        n0 = jnp.zeros((bn, nh, dh), F32)
        m0 = jnp.zeros((bn, nh), F32)
    else:
        k_past, v_past, c0, n0, m0, conv_prev = (a.astype(F32) for a in past)

    h, q, k, v, k_bf, v_bf = _proj_attn(x, mod, wts["g_norm1"], wts["w_in"], wts["g_q"], wts["g_k"])
    prev8 = jnp.pad(conv_prev, ((0, 0), (SUBLANES - (CONV_W - 1), 0), (0, 0)))
    raw, vo, gc, gr = _proj_mlstm(h, wts["w_in"], wts["w_gate"], wts["w_gate_t"])
    o_a = _attn(q, k_bf, v_bf, k_past, v_past)
    o_b, c_new, n_new, m_new, conv_new = _mlstm(
        raw, vo, gc, gr, wts["gate_bias_c"], wts["gate_bias_r"], wts["g_h"], c0, n0.reshape(bn, nh, 1, dh),
        jnp.broadcast_to(m0[:, :, None, None], (bn, nh, 1, LANES)), prev8, wts["w_conv"], wts["b_conv"])

    x, h2 = _out_proj(x, o_a, o_b, mod, wts["w_out"], wts["g_norm2"])
    x = _ffn(x, h2, mod, wts["w_ff1"], wts["w_ff2"])
    return x, (k, v, c_new, n_new.reshape(bn, nh, dh), m_new[:, :, 0, 0], conv_new)


def _layer_weights(l, w_in, g_norm1, g_q, g_k, w_conv, b_conv, b_i, b_f, g_h, w_out, g_norm2, w_ff1, w_ff2):
    d = w_in.shape[1]
    nh = N_HEADS_B
    w_gate = w_in[l][:, w_in.shape[2] - 2 * nh:]
    bias = jnp.concatenate([b_i[l], b_f[l]])
    rows = 2 * SUBLANES
    return {
        "w_in": jnp.swapaxes(w_in[l], 0, 1).astype(BF16),
        "w_gate": jnp.pad(w_gate, ((0, 0), (0, LANES - 2 * nh))).astype(BF16),
        "w_gate_t": jnp.pad(w_gate.T, ((0, rows - 2 * nh), (0, 0))).astype(BF16),
        "gate_bias_c": jnp.pad(bias, (0, LANES - 2 * nh)).reshape(1, LANES),
        "gate_bias_r": jnp.pad(bias, (0, rows - 2 * nh)).reshape(rows, 1),
        "g_norm1": g_norm1[l].reshape(1, d), "g_norm2": g_norm2[l].reshape(1, d),
        "g_q": g_q[l].reshape(1, -1), "g_k": g_k[l].reshape(1, -1),
        "w_conv": w_conv[l], "b_conv": b_conv[l].reshape(1, -1),
        "g_h": g_h[l].reshape(nh, 1, -1),
        "w_out": w_out[l].astype(BF16), "w_ff1": w_ff1[l].astype(BF16), "w_ff2": w_ff2[l].astype(BF16),
    }


def kernel(x_prompt, x_sample, c_prompt, c_sample, cache_k, cache_v, state_C, state_n, state_m, state_conv,
           w_ada, b_ada, g_norm1, w_in, g_q, g_k, w_conv, b_conv, b_i, b_f, g_h, w_out, g_norm2, w_ff1, w_ff2):
    depth = w_ada.shape[0]
    bp, bs = c_prompt.shape[0], c_sample.shape[0]
    c_rows = -(-(bp + bs) // (2 * SUBLANES)) * (2 * SUBLANES)
    c_all = jnp.pad(jnp.concatenate([c_prompt, c_sample], axis=0), ((0, c_rows - bp - bs), (0, 0)))
    y_prompt, y_sample = x_prompt, x_sample
    new_p, new_s = [], []
    for l in range(depth):
        wts = _layer_weights(l, w_in, g_norm1, g_q, g_k, w_conv, b_conv, b_i, b_f, g_h, w_out, g_norm2,
                             w_ff1, w_ff2)
        mod = _ada(c_all, w_ada[l], b_ada[l].reshape(1, -1))
        mod_p = mod[:bp].reshape(bp, 1, -1)
        mod_s = mod[bp:bp + bs].reshape(bs, 1, -1)
        y_prompt, sp = _layer(y_prompt, mod_p, None, wts)
        past = (cache_k[l], cache_v[l], state_C[l], state_n[l], state_m[l], state_conv[l])
        y_sample, ss = _layer(y_sample, mod_s, past, wts)
        new_p.append(sp)
        new_s.append(ss)
    stack = lambda states, i: jnp.stack([s[i] for s in states], axis=0)
    return ((y_prompt, y_sample) + tuple(stack(new_p, i) for i in range(6))
            + tuple(stack(new_s, i) for i in range(6)))
```

```python
import functools

import jax
import jax.numpy as jnp
from jax import lax
from jax.experimental import pallas as pl
from jax.experimental.pallas import tpu as pltpu

F32, BF16 = jnp.float32, jnp.bfloat16
EPS = 1e-6
HEAD_DIM_A = 128
N_HEADS_B = 4
CONV_W = 4
LANES = 128
SUBLANES = 8
F32_EXP2_UNDERFLOW = 151.0
LOG2_E = 1.4426950408889634
NO_PAST = 1e30
V7X_VMEM_BYTES = 64 * 2 ** 20
VMEM_LIMIT = V7X_VMEM_BYTES * 7 // 8

ROW_TILE = 512
ATTN_BLOCK = 256
SAMPLE_KEY_BLOCK = 256
MLSTM_CHUNK = 256
FF_TILE = 1024
ADA_TILE = 1024
ATTN_HEADS_PER_STEP = 4
SUFFIX_SUM_TERMS = 1
PROJ_COLS = 256


def _params(semantics, vmem_limit=VMEM_LIMIT):
    return pltpu.CompilerParams(dimension_semantics=semantics, vmem_limit_bytes=vmem_limit)


def _resident(shape, index_map):
    return pl.BlockSpec(shape, index_map, pipeline_mode=pl.Buffered(1))


def _sigmoid(x):
    return 0.5 * jnp.tanh(0.5 * x) + 0.5


def _log_sigmoid(x):
    return jnp.minimum(x, 0.0) - jnp.log1p(jnp.exp(-jnp.abs(x)))


def _dot(a, b):
    return jnp.dot(a, b, preferred_element_type=F32)


def _dot_nt(a, b):
    return lax.dot_general(a, b, (((1,), (1,)), ((), ())), preferred_element_type=F32)


def _dot_tn(a, b):
    return lax.dot_general(a, b, (((0,), (0,)), ((), ())), preferred_element_type=F32)


def _split_bf16(x, parts):
    out = []
    for _ in range(parts - 1):
        p = x.astype(BF16)
        out.append(p)
        x = x - p.astype(F32)
    out.append(x.astype(BF16))
    return out


def _modulated_norm(x, g, sc, sh):
    y = x * lax.rsqrt(jnp.mean(x * x, axis=-1, keepdims=True) + EPS)
    return y * (g * (1.0 + sc)) + sh


def _row_blocks(bn, t, rows):
    if t >= rows:
        return 1, rows
    return min(bn, rows // t), t


def _ada_kernel(c_ref, w_ref, b_ref, o_ref):
    c = c_ref[...]
    s = (c * _sigmoid(c)).astype(BF16)
    o_ref[...] = _dot(s, w_ref[...].astype(BF16)) + b_ref[...]


def _ada(c, w, b):
    m, d = c.shape
    n = w.shape[1]
    return pl.pallas_call(
        _ada_kernel,
        out_shape=jax.ShapeDtypeStruct((m, n), F32),
        grid=(n // ADA_TILE,),
        in_specs=[pl.BlockSpec((m, d), lambda j: (0, 0)),
                  pl.BlockSpec((d, ADA_TILE), lambda j: (0, j)),
                  pl.BlockSpec((1, ADA_TILE), lambda j: (0, j))],
        out_specs=pl.BlockSpec((m, ADA_TILE), lambda j: (0, j)),
        compiler_params=_params(("parallel",)),
        name="ada",
    )(c, w, b)


def _proj_attn_kernel(x_ref, sh_ref, sc_ref, g1_ref, w_ref, gq_ref, gk_ref, h_ref, q_ref, k_ref, v_ref, kb_ref,
                      vb_ref):
    bb, tt, d = x_ref.shape
    nh = q_ref.shape[1]
    da = nh * HEAD_DIM_A
    h3 = _modulated_norm(x_ref[...], g1_ref[...], sc_ref[...], sh_ref[...]).astype(BF16)
    h_ref[...] = h3
    hb = h3.reshape(bb * tt, d)
    outs = (((q_ref,), gq_ref), ((k_ref, kb_ref), gk_ref), ((v_ref, vb_ref), None))
    for part, (out_refs, g_ref) in enumerate(outs):
        u = _dot_nt(hb, w_ref[part * da:(part + 1) * da, :])
        for hd in range(nh):
            uh = u[:, hd * HEAD_DIM_A:(hd + 1) * HEAD_DIM_A]
            if g_ref is not None:
                uh = uh * lax.rsqrt(jnp.mean(uh * uh, axis=-1, keepdims=True) + EPS) * g_ref[...]
            for out_ref in out_refs:
                out_ref[:, hd] = uh.reshape(bb, tt, HEAD_DIM_A).astype(out_ref.dtype)


def _proj_attn(x, mod, g1, w, gq, gk):
    bn, t, d = x.shape
    nh = d // 2 // HEAD_DIM_A
    bb, tt = _row_blocks(bn, t, ROW_TILE)
    head_spec = pl.BlockSpec((bb, nh, tt, HEAD_DIM_A), lambda b, i: (b, 0, i, 0))
    return pl.pallas_call(
        _proj_attn_kernel,
        out_shape=(jax.ShapeDtypeStruct((bn, t, d), BF16),) + tuple(
            jax.ShapeDtypeStruct((bn, nh, t, HEAD_DIM_A), dt) for dt in (BF16, F32, F32, BF16, BF16)),
        grid=(bn // bb, t // tt),
        in_specs=[pl.BlockSpec((bb, tt, d), lambda b, i: (b, i, 0)),
                  pl.BlockSpec((bb, 1, d), lambda b, i: (b, 0, 0)),
                  pl.BlockSpec((bb, 1, d), lambda b, i: (b, 0, 1)),
                  pl.BlockSpec((1, d), lambda b, i: (0, 0)),
                  _resident((3 * nh * HEAD_DIM_A, d), lambda b, i: (0, 0)),
                  pl.BlockSpec((1, HEAD_DIM_A), lambda b, i: (0, 0)),
                  pl.BlockSpec((1, HEAD_DIM_A), lambda b, i: (0, 0))],
        out_specs=(pl.BlockSpec((bb, tt, d), lambda b, i: (b, i, 0)),) + (head_spec,) * 5,
        compiler_params=_params(("parallel", "parallel")),
        name="proj_attn",
    )(x, mod, mod, g1, w, gq, gk)


def _proj_mlstm_kernel(h_ref, wq_ref, wk_ref, wv_ref, wo_ref, wg_ref, wgt_ref, raw_ref, vo_ref, gc_ref, gr_ref):
    bb, tt, d = h_ref.shape
    c = raw_ref.shape[2]
    hb = h_ref[...].reshape(bb * tt, d)
    db = c // 2
    step = min(4 * PROJ_COLS, db)

    def w_cols(lo):
        return (wq_ref, wk_ref, wv_ref, wo_ref)[lo // db][lo % db:lo % db + step, :]

    for lo in range(0, c, step):
        raw_ref[:, :, lo:lo + step] = _dot_nt(hb, w_cols(lo)).reshape(bb, tt, step)
        vo_ref[:, :, lo:lo + step] = _dot_nt(hb, w_cols(c + lo)).reshape(bb, tt, step).astype(vo_ref.dtype)
    gc_ref[...] = _dot(hb, wg_ref[...]).reshape(bb, tt, LANES)
    for b in range(bb):
        gr_ref[b] = _dot_nt(wgt_ref[...], hb[b * tt:(b + 1) * tt])


def _proj_mlstm(h, w, wg, wgt):
    bn, t, d = h.shape
    db = d - d // 2
    c = 2 * db
    first = (3 * (d // 2)) // db
    bb, tt = _row_blocks(bn, t, ROW_TILE)
    return pl.pallas_call(
        _proj_mlstm_kernel,
        out_shape=(jax.ShapeDtypeStruct((bn, t, c), F32),
                   jax.ShapeDtypeStruct((bn, t, c), BF16),
                   jax.ShapeDtypeStruct((bn, t, LANES), F32),
                   jax.ShapeDtypeStruct((bn, wgt.shape[0], t), F32)),
        grid=(bn // bb, t // tt),
        in_specs=[pl.BlockSpec((bb, tt, d), lambda b, i: (b, i, 0)),
                  *[_resident((db, d), lambda b, i, j=j: (first + j, 0)) for j in range(4)],
                  _resident(wg.shape, lambda b, i: (0, 0)),
                  _resident(wgt.shape, lambda b, i: (0, 0))],
        out_specs=(pl.BlockSpec((bb, tt, c), lambda b, i: (b, i, 0)),
                   pl.BlockSpec((bb, tt, c), lambda b, i: (b, i, 0)),
                   pl.BlockSpec((bb, tt, LANES), lambda b, i: (b, i, 0)),
                   pl.BlockSpec((bb, wgt.shape[0], tt), lambda b, i: (b, 0, i))),
        compiler_params=_params(("parallel", "parallel")),
        name="proj_mlstm",
    )(h, w, w, w, w, wg, wgt)


def _strict_lower_ones(n):
    r = lax.broadcasted_iota(jnp.int32, (n, n), 0)
    c = lax.broadcasted_iota(jnp.int32, (n, n), 1)
    return (r > c).astype(BF16)


def _sb_block(q, kb, vb, tri, carry, mask):
    z = _dot_nt(q, kb) * (HEAD_DIM_A ** -0.5 * LOG2_E)
    log_beta = jnp.minimum(z, 0.0) - jnp.log2(1.0 + jnp.exp2(-jnp.abs(z)))
    log_1m = log_beta - z
    if mask is not None:
        log_1m = jnp.where(mask, log_1m, 0.0)
    tail = sum(_dot(p, tri) for p in _split_bf16(log_1m, SUFFIX_SUM_TERMS))
    total = tail[:, :1] + log_1m[:, :1]
    if carry is not None:
        tail = tail + carry
    a = jnp.exp2(log_beta + tail)
    if mask is not None:
        a = jnp.where(mask, a, 0.0)
    return _dot(a.astype(BF16), vb), total


def _attn_kernel(q_ref, kd_ref, vd_ref, kp_ref, vp_ref, o_ref, *maybe_carry_ref, tk, past_is_prefix):
    hps, tq, dh = q_ref.shape[1], q_ref.shape[2], q_ref.shape[3]
    row = lax.broadcasted_iota(jnp.int32, (tq, tq), 0)
    col = lax.broadcasted_iota(jnp.int32, (tq, tq), 1)
    n_past = pl.program_id(2) * (tq // tk) if past_is_prefix else kp_ref.shape[2] // tk
    tri = _strict_lower_ones(tk)
    tri_diag = tri if tq == tk else _strict_lower_ones(tq)

    def past_block(h, jj, carry):
        start = pl.multiple_of(jnp.maximum(n_past - 1 - jj, 0) * tk, tk)
        kb = kp_ref[0, h, pl.ds(start, tk), :].astype(BF16)
        vb = vp_ref[0, h, pl.ds(start, tk), :].astype(BF16)
        return _sb_block(q_ref[0, h], kb, vb, tri, carry, None)

    state = []
    for h in range(hps):
        acc, s = _sb_block(q_ref[0, h], kd_ref[0, h].astype(BF16), vd_ref[0, h].astype(BF16), tri_diag, None,
                           col < row)
        pv, ds = past_block(h, 0, jnp.where(n_past > 0, s, -NO_PAST))
        state.append((acc + pv, s + ds))

    worst = None
    for h, (acc, s) in enumerate(state):
        def more(c):
            jj, _, s = c
            return jnp.logical_and(jj < n_past, jnp.max(s) > -F32_EXP2_UNDERFLOW)

        def body(c, h=h):
            jj, acc, s = c
            pv, ds = past_block(h, jj, s)
            return jj + 1, acc + pv, s + ds

        _, acc, s = lax.while_loop(more, body, (jnp.int32(1), acc, s))
        o_ref[0, :, h * dh:(h + 1) * dh] = acc.astype(o_ref.dtype)
        if maybe_carry_ref:
            top = jnp.max(s, axis=0, keepdims=True)
            worst = top if worst is None else jnp.maximum(worst, top)
    if maybe_carry_ref:
        carry_ref, = maybe_carry_ref
        carry_ref[0, 0, 0] = jnp.broadcast_to(worst, carry_ref.shape[3:])


def _attn_call(q, k, v, k_past, v_past, prefix, window):
    bn, nh, t, dh = q.shape
    if prefix:
        tq = tk = min(t, ATTN_BLOCK)
    else:
        tq, tk = t, min(k_past.shape[2], SAMPLE_KEY_BLOCK)
    p = k_past.shape[2] if window is None else window
    last = k_past.shape[2] // p - 1
    hps = min(nh, ATTN_HEADS_PER_STEP)
    blk = pl.BlockSpec((1, hps, tq, dh), lambda b, h, i: (b, h, i, 0))
    past = pl.BlockSpec((1, hps, p, dh), lambda b, h, i: (b, h, last, 0))
    out_shape = [jax.ShapeDtypeStruct((bn, t, nh * dh), BF16)]
    out_specs = [pl.BlockSpec((1, tq, hps * dh), lambda b, h, i: (b, i, h))]
    if window is not None:
        out_shape.append(jax.ShapeDtypeStruct((bn, nh // hps, t // tq, SUBLANES, LANES), F32))
        out_specs.append(pl.BlockSpec((1, 1, 1, SUBLANES, LANES), lambda b, h, i: (b, h, i, 0, 0)))
    return pl.pallas_call(
        functools.partial(_attn_kernel, tk=tk, past_is_prefix=prefix),
        out_shape=tuple(out_shape),
        grid=(bn, nh // hps, t // tq),
        in_specs=[blk, blk, blk, past, past],
        out_specs=tuple(out_specs),
        compiler_params=_params(("parallel", "parallel", "parallel")),
        name="attn",
    )(q, k, v, k_past, v_past)


def _attn(q, k, v, k_past, v_past):
    if k_past is None:
        return _attn_call(q, k, v, k, v, True, None)[0]
    near, carry = _attn_call(q, k, v, k_past, v_past, False, min(k_past.shape[2], SAMPLE_KEY_BLOCK))
    return lax.cond(jnp.max(carry) > -F32_EXP2_UNDERFLOW,
                    lambda: _attn_call(q, k, v, k_past, v_past, False, None)[0],
                    lambda: near)


def _mlstm_kernel(raw_ref, vo_ref, gc_ref, gr_ref, bc_ref, br_ref, gh_ref, c0_ref, n0_ref, m0_ref, prev_ref,
                  wc_ref, cb_ref, o_ref, c_out_ref, n_out_ref, m_out_ref, new_ref, c_scr, n_scr, m_scr, halo_scr):
    ci = pl.program_id(1)
    ln = raw_ref.shape[1]
    nh = N_HEADS_B
    dh = raw_ref.shape[2] // (2 * nh)

    @pl.when(ci == 0)
    def _():
        c_scr[...] = c0_ref[0]
        n_scr[...] = n0_ref[0]
        m_scr[...] = m0_ref[0]
        halo_scr[...] = prev_ref[0]

    def conv(cols, scale):
        u = raw_ref[0, :, cols]
        ext = jnp.concatenate([halo_scr[:, cols], u], axis=0)
        w = 0.5 * wc_ref[:, cols]
        h = pltpu.roll(ext, CONV_W - 1, axis=0)[SUBLANES:] * w[0:1]
        for j in range(1, CONV_W - 1):
            h = h + pltpu.roll(ext, CONV_W - 1 - j, axis=0)[SUBLANES:] * w[j:j + 1]
        h = h + u * w[CONV_W - 1:CONV_W] + 0.5 * cb_ref[:, cols]
        halo_scr[:, cols] = ext[ln:ln + SUBLANES]
        y = h + h * jnp.tanh(h)
        return (y if scale == 1.0 else y * scale).astype(BF16)

    gc = gc_ref[0] + bc_ref[...]
    gr = gr_ref[0] + br_ref[...]
    lfc = _log_sigmoid(gc)
    lfr = _log_sigmoid(gr)
    row = lax.broadcasted_iota(jnp.int32, (ln, ln), 0)
    col = lax.broadcasted_iota(jnp.int32, (ln, ln), 1)
    causal = col <= row
    ones_causal = causal.astype(BF16)
    ones_causal_t = (row <= col).astype(BF16)
    b_rows = sum(_dot(p, ones_causal_t) for p in _split_bf16(lfr, 3))

    for hd in range(nh):
        q = conv(slice(hd * dh, (hd + 1) * dh), 1.0)
        k = conv(slice((nh + hd) * dh, (nh + hd + 1) * dh), dh ** -0.5)
        v = vo_ref[0, :, hd * dh:(hd + 1) * dh]
        og = vo_ref[0, :, (nh + hd) * dh:(nh + hd + 1) * dh].astype(F32)
        ig_col = gc[:, hd:hd + 1]
        ig_row = gr[hd:hd + 1, :]
        b_row = b_rows[nh + hd:nh + hd + 1, :]
        lf_b = jnp.broadcast_to(lfc[:, nh + hd:nh + hd + 1], (ln, ln))
        bt = sum(_dot(ones_causal, p) for p in _split_bf16(lf_b, 3))
        b_col = bt[:, :1]
        m_prev = m_scr[hd][:, :1]
        c_prev = c_scr[hd]
        n_prev = n_scr[hd]

        dmat = jnp.where(causal, bt - (b_row - ig_row), -jnp.inf)
        inter = b_col + m_prev
        m_t = jnp.maximum(inter, jnp.max(dmat, axis=1, keepdims=True))
        w_inter = jnp.exp(inter - m_t)
        s = jnp.exp(dmat - m_t) * _dot_nt(q, k)
        num = w_inter * _dot(q, c_prev.astype(BF16)) + _dot(s.astype(BF16), v)
        den = (w_inter * jnp.sum(q.astype(F32) * n_prev, axis=1, keepdims=True)
               + jnp.sum(s, axis=1, keepdims=True))
        h = num / jnp.maximum(jnp.abs(den), jnp.exp(-m_t))
        g = h * lax.rsqrt(jnp.mean(h * h, axis=1, keepdims=True) + EPS) * (0.5 * gh_ref[hd])
        o_ref[0, :, hd * dh:(hd + 1) * dh] = (g + g * jnp.tanh(0.5 * og)).astype(o_ref.dtype)

        b_last = b_col[ln - 1:ln]
        m_new = m_t[ln - 1:ln]
        w_state = jnp.exp((b_last - b_col) + ig_col - m_new)
        decay = jnp.exp(b_last + m_prev - m_new)
        ks = k.astype(F32) * w_state
        c_scr[hd] = decay * c_prev + _dot_tn(ks.astype(BF16), v)
        n_scr[hd] = decay * n_prev + jnp.sum(ks, axis=0, keepdims=True)
        m_scr[hd] = jnp.broadcast_to(m_new, (1, LANES))

    @pl.when(ci == pl.num_programs(1) - 1)
    def _():
        c_out_ref[0] = c_scr[...]
        n_out_ref[0] = n_scr[...]
        m_out_ref[0] = m_scr[...]
        new_ref[0] = halo_scr[SUBLANES - (CONV_W - 1):SUBLANES]


def _mlstm(raw, vo, gc, gr, bias_c, bias_r, gh, c0, n0, m0, prev8, wc, cb):
    bn, t, c2 = raw.shape
    nh = N_HEADS_B
    dh = c2 // (2 * nh)
    ln = min(t, MLSTM_CHUNK)
    rows = gr.shape[1]
    state = lambda shape: pl.BlockSpec((1,) + shape, lambda b, i: (b, 0, 0, 0))
    return pl.pallas_call(
        _mlstm_kernel,
        out_shape=(jax.ShapeDtypeStruct((bn, t, nh * dh), BF16),
                   jax.ShapeDtypeStruct((bn, nh, dh, dh), F32),
                   jax.ShapeDtypeStruct((bn, nh, 1, dh), F32),
                   jax.ShapeDtypeStruct((bn, nh, 1, LANES), F32),
                   jax.ShapeDtypeStruct((bn, CONV_W - 1, c2), F32)),
        grid=(bn, t // ln),
        in_specs=[pl.BlockSpec((1, ln, c2), lambda b, i: (b, i, 0)),
                  pl.BlockSpec((1, ln, c2), lambda b, i: (b, i, 0)),
                  pl.BlockSpec((1, ln, LANES), lambda b, i: (b, i, 0)),
                  pl.BlockSpec((1, rows, ln), lambda b, i: (b, 0, i)),
                  pl.BlockSpec((1, LANES), lambda b, i: (0, 0)),
                  pl.BlockSpec((rows, 1), lambda b, i: (0, 0)),
                  pl.BlockSpec((nh, 1, dh), lambda b, i: (0, 0, 0)),
                  state((nh, dh, dh)), state((nh, 1, dh)), state((nh, 1, LANES)),
                  pl.BlockSpec((1, SUBLANES, c2), lambda b, i: (b, 0, 0)),
                  pl.BlockSpec((CONV_W, c2), lambda b, i: (0, 0)),
                  pl.BlockSpec((1, c2), lambda b, i: (0, 0))],
        out_specs=(pl.BlockSpec((1, ln, nh * dh), lambda b, i: (b, i, 0)),
                   state((nh, dh, dh)), state((nh, 1, dh)), state((nh, 1, LANES)),
                   pl.BlockSpec((1, CONV_W - 1, c2), lambda b, i: (b, 0, 0))),
        scratch_shapes=[pltpu.VMEM((nh, dh, dh), F32), pltpu.VMEM((nh, 1, dh), F32),
                        pltpu.VMEM((nh, 1, LANES), F32), pltpu.VMEM((SUBLANES, c2), F32)],
        compiler_params=_params(("parallel", "arbitrary")),
        name="mlstm",
    )(raw, vo, gc, gr, bias_c, bias_r, gh, c0, n0, m0, prev8, wc, cb)


def _out_proj_kernel(x_ref, oa_ref, ob_ref, gt_ref, w_ref, sh_ref, sc_ref, g2_ref, o_ref, h_ref):
    bb, tt, d = x_ref.shape
    da, db = oa_ref.shape[2], ob_ref.shape[2]
    y = (_dot(oa_ref[...].reshape(bb * tt, da), w_ref[0:da])
         + _dot(ob_ref[...].reshape(bb * tt, db), w_ref[da:da + db]))
    x1 = x_ref[...] + gt_ref[...] * y.reshape(bb, tt, d)
    o_ref[...] = x1
    h_ref[...] = _modulated_norm(x1, g2_ref[...], sc_ref[...], sh_ref[...]).astype(BF16)


def _out_proj(x, oa, ob, mod, w, g2):
    bn, t, d = x.shape
    bb, tt = _row_blocks(bn, t, ROW_TILE)
    rows = lambda width: pl.BlockSpec((bb, tt, width), lambda b, i: (b, i, 0))
    return pl.pallas_call(
        _out_proj_kernel,
        out_shape=(jax.ShapeDtypeStruct((bn, t, d), F32), jax.ShapeDtypeStruct((bn, t, d), BF16)),
        grid=(bn // bb, t // tt),
        in_specs=[rows(d), rows(oa.shape[2]), rows(ob.shape[2]),
                  pl.BlockSpec((bb, 1, d), lambda b, i: (b, 0, 2)),
                  _resident(w.shape, lambda b, i: (0, 0)),
                  pl.BlockSpec((bb, 1, d), lambda b, i: (b, 0, 3)),
                  pl.BlockSpec((bb, 1, d), lambda b, i: (b, 0, 4)),
                  pl.BlockSpec((1, d), lambda b, i: (0, 0))],
        out_specs=(rows(d), rows(d)),
        compiler_params=_params(("parallel", "parallel")),
        name="out_proj",
    )(x, oa, ob, mod, w, mod, mod, g2)


def _ffn_kernel(x_ref, h_ref, gt_ref, w1_ref, w2_ref, o_ref, acc_scr):
    f = pl.program_id(2)
    bb, tt, d = x_ref.shape

    a = jnp.maximum(_dot(h_ref[...].reshape(bb * tt, d), w1_ref[...]), 0.0)
    acc = jnp.where(f == 0, 0.0, acc_scr[...]) + _dot((a * a).astype(BF16), w2_ref[...])
    acc_scr[...] = acc
    o_ref[...] = x_ref[...] + gt_ref[...] * acc.reshape(bb, tt, d)


def _ffn(x, h, mod, w1, w2):
    bn, t, d = x.shape
    dff = w1.shape[1]
    bb, tt = _row_blocks(bn, t, ROW_TILE)
    rows = pl.BlockSpec((bb, tt, d), lambda b, i, f: (b, i, 0))
    chunk = lambda c: pl.BlockSpec((bb, 1, d), lambda b, i, f: (b, 0, c))
    return pl.pallas_call(
        _ffn_kernel,
        out_shape=jax.ShapeDtypeStruct((bn, t, d), F32),
        grid=(bn // bb, t // tt, dff // FF_TILE),
        in_specs=[rows, rows, chunk(5),
                  pl.BlockSpec((d, FF_TILE), lambda b, i, f: (0, f)),
                  pl.BlockSpec((FF_TILE, d), lambda b, i, f: (f, 0))],
        out_specs=rows,
        scratch_shapes=[pltpu.VMEM((bb * tt, d), F32)],
        compiler_params=_params(("parallel", "parallel", "arbitrary")),
        name="ffn",
    )(x, h, mod, w1, w2)


def _layer(x, mod, past, wts):
    bn, t, d = x.shape
    db = d - d // 2
    nh, dh = N_HEADS_B, db // N_HEADS_B
    if past is None:
        k_past = v_past = None
        conv_prev = jnp.zeros((bn, CONV_W - 1, 2 * db), F32)
        c0 = jnp.zeros((bn, nh, dh, dh), F32)
        n0 = jnp.zeros((bn, nh, dh), F32)
        m0 = jnp.zeros((bn, nh), F32)
    else:
        k_past, v_past, c0, n0, m0, conv_prev = (a.astype(F32) for a in past)

    h, q, k, v, k_bf, v_bf = _proj_attn(x, mod, wts["g_norm1"], wts["w_in"], wts["g_q"], wts["g_k"])
    prev8 = jnp.pad(conv_prev, ((0, 0), (SUBLANES - (CONV_W - 1), 0), (0, 0)))
    raw, vo, gc, gr = _proj_mlstm(h, wts["w_in"], wts["w_gate"], wts["w_gate_t"])
    o_a = _attn(q, k_bf, v_bf, k_past, v_past)
    o_b, c_new, n_new, m_new, conv_new = _mlstm(
        raw, vo, gc, gr, wts["gate_bias_c"], wts["gate_bias_r"], wts["g_h"], c0, n0.reshape(bn, nh, 1, dh),
        jnp.broadcast_to(m0[:, :, None, None], (bn, nh, 1, LANES)), prev8, wts["w_conv"], wts["b_conv"])

    x, h2 = _out_proj(x, o_a, o_b, mod, wts["w_out"], wts["g_norm2"])
    x = _ffn(x, h2, mod, wts["w_ff1"], wts["w_ff2"])
    return x, (k, v, c_new, n_new.reshape(bn, nh, dh), m_new[:, :, 0, 0], conv_new)


def _layer_weights(l, w_in, g_norm1, g_q, g_k, w_conv, b_conv, b_i, b_f, g_h, w_out, g_norm2, w_ff1, w_ff2):
    d = w_in.shape[1]
    nh = N_HEADS_B
    w_gate = w_in[l][:, w_in.shape[2] - 2 * nh:]
    bias = jnp.concatenate([b_i[l], b_f[l]])
    rows = 2 * SUBLANES
    return {
        "w_in": jnp.swapaxes(w_in[l], 0, 1).astype(BF16),
        "w_gate": jnp.pad(w_gate, ((0, 0), (0, LANES - 2 * nh))).astype(BF16),
        "w_gate_t": jnp.pad(w_gate.T, ((0, rows - 2 * nh), (0, 0))).astype(BF16),
        "gate_bias_c": jnp.pad(bias, (0, LANES - 2 * nh)).reshape(1, LANES),
        "gate_bias_r": jnp.pad(bias, (0, rows - 2 * nh)).reshape(rows, 1),
        "g_norm1": g_norm1[l].reshape(1, d), "g_norm2": g_norm2[l].reshape(1, d),
        "g_q": g_q[l].reshape(1, -1), "g_k": g_k[l].reshape(1, -1),
        "w_conv": w_conv[l], "b_conv": b_conv[l].reshape(1, -1),
        "g_h": g_h[l].reshape(nh, 1, -1),
        "w_out": w_out[l].astype(BF16), "w_ff1": w_ff1[l].astype(BF16), "w_ff2": w_ff2[l].astype(BF16),
    }


def kernel(x_prompt, x_sample, c_prompt, c_sample, cache_k, cache_v, state_C, state_n, state_m, state_conv,
           w_ada, b_ada, g_norm1, w_in, g_q, g_k, w_conv, b_conv, b_i, b_f, g_h, w_out, g_norm2, w_ff1, w_ff2):
    depth = w_ada.shape[0]
    bp, bs = c_prompt.shape[0], c_sample.shape[0]
    c_rows = -(-(bp + bs) // (2 * SUBLANES)) * (2 * SUBLANES)
    c_all = jnp.pad(jnp.concatenate([c_prompt, c_sample], axis=0), ((0, c_rows - bp - bs), (0, 0)))
    y_prompt, y_sample = x_prompt, x_sample
    new_p, new_s = [], []
    for l in range(depth):
        wts = _layer_weights(l, w_in, g_norm1, g_q, g_k, w_conv, b_conv, b_i, b_f, g_h, w_out, g_norm2,
                             w_ff1, w_ff2)
        mod = _ada(c_all, w_ada[l], b_ada[l].reshape(1, -1))
        mod_p = mod[:bp].reshape(bp, 1, -1)
        mod_s = mod[bp:bp + bs].reshape(bs, 1, -1)
        y_prompt, sp = _layer(y_prompt, mod_p, None, wts)
        past = (cache_k[l], cache_v[l], state_C[l], state_n[l], state_m[l], state_conv[l])
        y_sample, ss = _layer(y_sample, mod_s, past, wts)
        new_p.append(sp)
        new_s.append(ss)
    stack = lambda states, i: jnp.stack([s[i] for s in states], axis=0)
    return ((y_prompt, y_sample) + tuple(stack(new_p, i) for i in range(6))
            + tuple(stack(new_s, i) for i in range(6)))
```
